```python
import jax, jax.numpy as jnp
from jax import lax
import numpy as np

D_MODEL = 1024
BATCH = 16
SEQ = 2048
DEPTH = 1

W_RWKV = D_MODEL // 2
HD_RWKV = 64
H_RWKV = W_RWKV // HD_RWKV
LORA_DECAY = 32
LORA_ICLR = 32
LORA_GATE = 96
GN_EPS = 64e-5

W_GLA = D_MODEL - W_RWKV
DV_GLA = 128
H_GLA = W_GLA // DV_GLA
DK_GLA = DV_GLA // 2
LORA_GK = 16
GK_NORMALIZER = 16.0
GLA_CHUNK = 64
GLA_EPS = 1e-5

D_FF = ((8 * D_MODEL // 3 + 255) // 256) * 256
RMS_EPS = 1e-6

RWKV_SPLITS = (W_RWKV, W_RWKV, W_RWKV, LORA_DECAY, LORA_ICLR, LORA_GATE)
GLA_SPLITS = (H_GLA * DK_GLA, H_GLA * DK_GLA, W_GLA, W_GLA, LORA_GK)
D_RWKV_IN = sum(RWKV_SPLITS)
D_GLA_IN = sum(GLA_SPLITS)
D_IN = D_RWKV_IN + D_GLA_IN

kernel_name = "hymba_rwkv7_gla_swiglu"


def _split(z, sizes):
    idx = [int(i) for i in np.cumsum(sizes)[:-1]]
    return jnp.split(z, idx, axis=-1)


def rms_norm(x, g, eps=RMS_EPS):
    xf = x.astype(jnp.float32)
    y = xf * lax.rsqrt(jnp.mean(xf * xf, axis=-1, keepdims=True) + eps)
    return (y * g.astype(jnp.float32)).astype(x.dtype)


def token_shift(z):
    return jnp.pad(z[:, :-1], ((0, 0), (1, 0), (0, 0)))


def rwkv7_mixer(z, mu, decay_base, decay_up, iclr_base, iclr_up, gate_up,
                k_k, k_a, r_k, lnx_w, lnx_b):
    B, T, _ = z.shape
    z = z.astype(jnp.float32)
    z = z + (token_shift(z) - z) * mu
    r, k, v, w_lo, a_lo, g_lo = _split(z, RWKV_SPLITS)
    log_w = -jax.nn.softplus(-(decay_base + jnp.tanh(w_lo) @ decay_up)) - 0.5
    decay = jnp.exp(-jnp.exp(log_w))
    a = jax.nn.sigmoid(iclr_base + a_lo @ iclr_up)
    g = jax.nn.sigmoid(g_lo) @ gate_up

    def heads(t):
        return t.reshape(B, T, H_RWKV, HD_RWKV)

    kk = heads(k * k_k)
    kk = kk * lax.rsqrt(jnp.maximum(jnp.sum(kk * kk, axis=-1, keepdims=True), 1e-24))
    k = k * (1.0 + (a - 1.0) * k_a)
    r_h, k_h, v_h, w_h, a_h = heads(r), heads(k), heads(v), heads(decay), heads(a)
    rem = -kk
    add = kk * a_h

    def step(S, inp):
        r_t, w_t, k_t, v_t, a_t, b_t = inp
        sa = jnp.einsum('bhvk,bhk->bhv', S, a_t)
        S = (S * w_t[:, :, None, :] + sa[..., None] * b_t[:, :, None, :]
             + v_t[..., None] * k_t[:, :, None, :])
        y = jnp.einsum('bhvk,bhk->bhv', S, r_t)
        return S, y

    S0 = jnp.zeros((B, H_RWKV, HD_RWKV, HD_RWKV), jnp.float32)
    xs = (jnp.moveaxis(r_h, 1, 0), jnp.moveaxis(w_h, 1, 0), jnp.moveaxis(k_h, 1, 0),
          jnp.moveaxis(v_h, 1, 0), jnp.moveaxis(rem, 1, 0), jnp.moveaxis(add, 1, 0))
    _, y = lax.scan(step, S0, xs)
    y = jnp.moveaxis(y, 0, 1)
    mean = jnp.mean(y, axis=-1, keepdims=True)
    var = jnp.mean(jnp.square(y - mean), axis=-1, keepdims=True)
    y = ((y - mean) * lax.rsqrt(var + GN_EPS)).reshape(B, T, W_RWKV) * lnx_w + lnx_b
    bonus = jnp.sum(r_h * k_h * r_k, axis=-1, keepdims=True) * v_h
    return (y + bonus.reshape(B, T, W_RWKV)) * g


def gla_mixer(z, gk_up, gk_bias, norm_g):
    B, T, _ = z.shape
    C = GLA_CHUNK
    N = T // C
    z = z.astype(jnp.float32)
    q, k, v, og, gk_lo = _split(z, GLA_SPLITS)
    gk = jax.nn.log_sigmoid(gk_lo @ gk_up + gk_bias) / GK_NORMALIZER

    def chunks(t, d):
        return t.reshape(B, N, C, H_GLA, d).transpose(0, 3, 1, 2, 4)

    q = chunks(q, DK_GLA) * (DK_GLA ** -0.5)
    k = chunks(k, DK_GLA)
    v = chunks(v, DV_GLA)
    b = jnp.cumsum(chunks(gk, DK_GLA), axis=3)
    b_last = b[:, :, :, -1:, :]
    q_in = q * jnp.exp(b)
    k_in = k * jnp.exp(-b)
    k_end = k * jnp.exp(b_last - b)

    causal = jnp.tril(jnp.ones((C, C), dtype=bool))
    att = jnp.where(causal, jnp.einsum('bhncd,bhnsd->bhncs', q_in, k_in), 0.0)
    o = jnp.einsum('bhncs,bhnsv->bhncv', att, v)

    delta = jnp.einsum('bhncd,bhncv->bhndv', k_end, v)
    chunk_decay = jnp.exp(b_last[:, :, :, 0, :])

    def carry_step(S, inp):
        dec, dS = inp
        return S * dec[..., None] + dS, S

    S0 = jnp.zeros((B, H_GLA, DK_GLA, DV_GLA), jnp.float32)
    _, S_prev = lax.scan(carry_step, S0, (jnp.moveaxis(chunk_decay, 2, 0), jnp.moveaxis(delta, 2, 0)))
    S_prev = jnp.moveaxis(S_prev, 0, 2)
    o = o + jnp.einsum('bhncd,bhndv->bhncv', q_in, S_prev)

    o = o.transpose(0, 2, 3, 1, 4).reshape(B, T, H_GLA, DV_GLA)
    o = o * lax.rsqrt(jnp.mean(o * o, axis=-1, keepdims=True) + GLA_EPS) * norm_g
    return o.reshape(B, T, W_GLA) * jax.nn.silu(og)


def setup_inputs(seed: int = 0) -> dict:
    key = jax.random.key(seed)
    ks = jax.random.split(key, 24)
    L, D = DEPTH, D_MODEL
    nrm = jax.random.normal
    f32 = jnp.float32
    return {
        "x": nrm(ks[0], (BATCH, SEQ, D), f32),
        "rms1_g": 1.0 + 0.02 * nrm(ks[1], (L, D), f32),
        "w_in": nrm(ks[2], (L, D, D_IN), f32) * D ** -0.5,
        "mu_shift": jax.random.uniform(ks[3], (L, D_RWKV_IN), f32),
        "decay_base": jax.random.uniform(ks[4], (L, W_RWKV), f32, -5.0, 0.0),
        "decay_up": 0.1 * nrm(ks[5], (L, LORA_DECAY, W_RWKV), f32),
        "iclr_base": 0.5 * nrm(ks[6], (L, W_RWKV), f32),
        "iclr_up": 0.5 * nrm(ks[7], (L, LORA_ICLR, W_RWKV), f32) * LORA_ICLR ** -0.5,
        "gate_up": nrm(ks[8], (L, LORA_GATE, W_RWKV), f32) * LORA_GATE ** -0.5,
        "k_k": 0.85 + 0.05 * nrm(ks[9], (L, W_RWKV), f32),
        "k_a": 1.0 + 0.05 * nrm(ks[10], (L, W_RWKV), f32),
        "r_k": 0.1 * nrm(ks[11], (L, H_RWKV, HD_RWKV), f32),
        "lnx_w": 1.0 + 0.02 * nrm(ks[12], (L, W_RWKV), f32),
        "lnx_b": 0.02 * nrm(ks[13], (L, W_RWKV), f32),
        "gk_up": nrm(ks[14], (L, LORA_GK, H_GLA * DK_GLA), f32) * LORA_GK ** -0.5,
        "gk_bias": 0.1 * nrm(ks[15], (L, H_GLA * DK_GLA), f32),
        "gla_norm_g": 1.0 + 0.02 * nrm(ks[16], (L, DV_GLA), f32),
        "w_out": nrm(ks[17], (L, D, D), f32) * D ** -0.5,
        "rms2_g": 1.0 + 0.02 * nrm(ks[18], (L, D), f32),
        "ffn_gate": nrm(ks[19], (L, D, D_FF), f32) * D ** -0.5,
        "ffn_up": nrm(ks[20], (L, D, D_FF), f32) * D ** -0.5,
        "ffn_down": nrm(ks[21], (L, D_FF, D), f32) * D_FF ** -0.5,
        "final_g": 1.0 + 0.02 * nrm(ks[22], (D,), f32),
    }


def reference(x, rms1_g, w_in, mu_shift, decay_base, decay_up, iclr_base, iclr_up,
              gate_up, k_k, k_a, r_k, lnx_w, lnx_b, gk_up, gk_bias, gla_norm_g,
              w_out, rms2_g, ffn_gate, ffn_up, ffn_down, final_g):
    h = x
    for l in range(DEPTH):
        n = rms_norm(h, rms1_g[l])
        z = n @ w_in[l]
        y_rwkv = rwkv7_mixer(z[..., :D_RWKV_IN], mu_shift[l], decay_base[l], decay_up[l],
                             iclr_base[l], iclr_up[l], gate_up[l], k_k[l], k_a[l],
                             r_k[l], lnx_w[l], lnx_b[l])
        y_gla = gla_mixer(z[..., D_RWKV_IN:], gk_up[l], gk_bias[l], gla_norm_g[l])
        mix = jnp.concatenate([y_rwkv, y_gla], axis=-1).astype(h.dtype)
        h = h + mix @ w_out[l]
        n = rms_norm(h, rms2_g[l])
        h = h + (jax.nn.silu(n @ ffn_gate[l]) * (n @ ffn_up[l])) @ ffn_down[l]
    return rms_norm(h, final_g)
```

```python
import functools

import jax
import jax.numpy as jnp
from jax import lax
from jax.experimental import pallas as pl
from jax.experimental.pallas import tpu as pltpu

F32 = jnp.float32
BF16 = jnp.bfloat16

D_MODEL = 1024
W_RWKV = 512
HD_RWKV = 64
LORA_DECAY = 32
LORA_ICLR = 32
LORA_GATE = 96
GN_EPS = 64e-5
W_GLA = 512
DV_GLA = 128
H_GLA = 4
DK_GLA = 64
LORA_GK = 16
GK_NORMALIZER = 16.0
GLA_EPS = 1e-5
D_FF = 2816
RMS_EPS = 1e-6
CHUNK = 64

LANES = 128
PAIR_ROWS = 2 * CHUNK
LORA_PAD = 256
ZR_COLS = 3 * W_RWKV + LORA_PAD
GK_PAD = 128
ZG_COLS = 2 * H_GLA * DK_GLA + 2 * W_GLA + GK_PAD
QK_GLA = H_GLA * DK_GLA

TM_PROJ = 512
TB_MIX = 256
VMEM_LIMIT = 56 * 1024 * 1024


def _dot(a, b):
    return jnp.dot(a, b, preferred_element_type=F32)


def _dot_nt(a, b):
    return lax.dot_general(a, b, (((1,), (1,)), ((), ())), preferred_element_type=F32)


def _dot_tn(a, b):
    return lax.dot_general(a, b, (((0,), (0,)), ((), ())), preferred_element_type=F32)


def _split3(x):
    p0 = x.astype(BF16)
    r = x - p0.astype(F32)
    p1 = r.astype(BF16)
    p2 = (r - p1.astype(F32)).astype(BF16)
    return p0, p1, p2


def _dot_exact_rhs(m01, x):
    p0, p1, p2 = _split3(x)
    return _dot(m01, p0) + _dot(m01, p1) + _dot(m01, p2)


def _dot_exact_lhs(x, m01):
    p0, p1, p2 = _split3(x)
    return _dot(p0, m01) + _dot(p1, m01) + _dot(p2, m01)


def _sigmoid(x):
    return 1.0 / (1.0 + jnp.exp(-x))


def _softplus(x):
    return jnp.maximum(x, 0.0) + jnp.log(1.0 + jnp.exp(-jnp.abs(x)))


def _stack_heads(x, lane_is_h0):
    zero = jnp.zeros_like(x)
    return jnp.concatenate([jnp.where(lane_is_h0, x, zero), jnp.where(lane_is_h0, zero, x)], axis=0)


def _inproj_kernel(x_ref, g_ref, wr_ref, wg_ref, zr_ref, zg_ref):
    x = x_ref[...]
    n = x * lax.rsqrt(jnp.mean(x * x, axis=-1, keepdims=True) + RMS_EPS) * g_ref[...]
    n = n.astype(BF16)
    zr_ref[...] = _dot(n, wr_ref[...])
    zg_ref[...] = _dot(n, wg_ref[...])


def _const_spec(shape):
    return pl.BlockSpec(shape, lambda *_: (0,) * len(shape), pipeline_mode=pl.Buffered(1))


def _inproj(x2d, g, w_r, w_g):
    n_tok = x2d.shape[0]
    return pl.pallas_call(
        _inproj_kernel,
        grid=(n_tok // TM_PROJ,),
        in_specs=[
            pl.BlockSpec((TM_PROJ, D_MODEL), lambda i: (i, 0)),
            _const_spec((1, D_MODEL)),
            _const_spec((D_MODEL, ZR_COLS)),
            _const_spec((D_MODEL, ZG_COLS)),
        ],
        out_specs=[
            pl.BlockSpec((TM_PROJ, ZR_COLS), lambda i: (i, 0)),
            pl.BlockSpec((TM_PROJ, ZG_COLS), lambda i: (i, 0)),
        ],
        out_shape=[
            jax.ShapeDtypeStruct((n_tok, ZR_COLS), F32),
            jax.ShapeDtypeStruct((n_tok, ZG_COLS), F32),
        ],
        compiler_params=pltpu.CompilerParams(
            dimension_semantics=("parallel",), vmem_limit_bytes=VMEM_LIMIT),
        name="inproj",
    )(x2d, g, w_r, w_g)


def _rwkv_kernel(z_ref, mu_ref, db_ref, du_ref, ib_ref, iu_ref, gu_ref, kk_ref, ka_ref, rk_ref,
                 lw_ref, lb_ref, bd_ref, ltri_ref, lones_ref,
                 y_ref,
                 zprev_ref, s_ref, at_ref, rt_ref, bc_ref, kc_ref, bh_ref, kh_ref, v_ref, wc_ref, ybuf_ref):
    tb = z_ref.shape[0]
    n_chunks = tb // CHUNK
    n_pairs = W_RWKV // LANES

    @pl.when(pl.program_id(1) == 0)
    def _():
        zprev_ref[...] = jnp.zeros_like(zprev_ref)
        s_ref[...] = jnp.zeros_like(s_ref)

    z = z_ref[...]
    row = lax.broadcasted_iota(jnp.int32, (tb, 1), 0)
    z_prev = jnp.where(row == 0, zprev_ref[...], pltpu.roll(z, 1, 0))
    zprev_ref[...] = z[tb - 1:tb, :]
    zs = z + (z_prev - z) * mu_ref[...]

    r = zs[:, 0:W_RWKV]
    k = zs[:, W_RWKV:2 * W_RWKV]
    v = zs[:, 2 * W_RWKV:3 * W_RWKV]
    lo = zs[:, 3 * W_RWKV:ZR_COLS]

    bd = bd_ref[...]
    xw = db_ref[...] + _dot(jnp.tanh(lo).astype(BF16), du_ref[...])
    ld = -jnp.exp(-_softplus(-xw) - 0.5)
    a = _sigmoid(ib_ref[...] + _dot(lo.astype(BF16), iu_ref[...]))
    g = _dot(_sigmoid(lo).astype(BF16), gu_ref[...])

    kk = k * kk_ref[...]
    kk = kk * lax.rsqrt(jnp.maximum(_dot_exact_lhs(kk * kk, bd), 1e-24))
    k2 = k * (1.0 + (a - 1.0) * ka_ref[...])
    bonus = _dot_exact_lhs(r * k2 * rk_ref[...], bd) * v
    av = -kk
    bv = kk * a

    cw = _dot_exact_rhs(ltri_ref[...], ld)
    cw_last = _dot_exact_rhs(lones_ref[...], ld)
    e_neg = jnp.exp(-cw)
    e_rem = jnp.exp(cw_last - cw)
    at_ref[...] = (av * jnp.exp(cw - ld)).astype(BF16)
    rt_ref[...] = (r * jnp.exp(cw)).astype(BF16)
    bc_ref[...] = (bv * e_neg).astype(BF16)
    kc_ref[...] = (k2 * e_neg).astype(BF16)
    bh_ref[...] = (bv * e_rem).astype(BF16)
    kh_ref[...] = (k2 * e_rem).astype(BF16)
    v_ref[...] = v.astype(BF16)
    wc_ref[...] = jnp.exp(cw_last)

    lane_is_h0 = lax.broadcasted_iota(jnp.int32, (CHUNK, LANES), 1) < HD_RWKV
    ri = lax.broadcasted_iota(jnp.int32, (PAIR_ROWS, PAIR_ROWS), 0)
    ci = lax.broadcasted_iota(jnp.int32, (PAIR_ROWS, PAIR_ROWS), 1)
    same_head = (ri < CHUNK) == (ci < CHUNK)
    strict_lower = same_head & (ri > ci)
    incl_lower = same_head & (ri >= ci)
    eye = (ri == ci).astype(F32)

    for n in range(n_chunks):
        rows = pl.ds(n * CHUNK, CHUNK)
        for p in range(n_pairs):
            lanes = pl.ds(p * LANES, LANES)
            at_s = _stack_heads(at_ref[rows, lanes], lane_is_h0)
            rt_s = _stack_heads(rt_ref[rows, lanes], lane_is_h0)
            bc_s = _stack_heads(bc_ref[rows, lanes], lane_is_h0)
            kc_s = _stack_heads(kc_ref[rows, lanes], lane_is_h0)
            v_p = v_ref[rows, lanes]
            v_s = _stack_heads(v_p, lane_is_h0)

            s1 = _dot_nt(jnp.concatenate([at_s, rt_s], axis=0), jnp.concatenate([bc_s, kc_s], axis=0))
            a_ab = jnp.where(strict_lower, s1[0:PAIR_ROWS, 0:PAIR_ROWS], 0.0)
            a_ak = jnp.where(strict_lower, s1[0:PAIR_ROWS, PAIR_ROWS:], 0.0)
            a_rb = jnp.where(incl_lower, s1[PAIR_ROWS:, 0:PAIR_ROWS], 0.0)
            a_rk = jnp.where(incl_lower, s1[PAIR_ROWS:, PAIR_ROWS:], 0.0)

            t_inv = eye + a_ab
            a_pow = a_ab.astype(BF16)
            for _ in range(5):
                a_pow = _dot(a_pow, a_pow).astype(BF16)
                t_inv = t_inv + _dot(t_inv.astype(BF16), a_pow)

            av_both = _dot(jnp.concatenate([a_ak, a_rk], axis=0).astype(BF16), v_s)
            akv = av_both[0:PAIR_ROWS]
            arkv = av_both[PAIR_ROWS:]
            pq = _dot(t_inv.astype(BF16), jnp.concatenate([at_s, akv.astype(BF16)], axis=1))
            p_m = pq[:, 0:LANES]
            q_m = pq[:, LANES:]
            g_m = _dot_tn(v_p, kh_ref[rows, lanes])

            s_old = s_ref[p]
            s_bf = s_old.astype(BF16)
            u = _dot_nt(p_m.astype(BF16), s_bf) + q_m
            y_s = _dot_nt(rt_s, s_bf) + _dot(a_rb.astype(BF16), u.astype(BF16)) + arkv
            ybuf_ref[rows, lanes] = y_s[0:CHUNK] + y_s[CHUNK:]
            u_p = (u[0:CHUNK] + u[CHUNK:]).astype(BF16)
            ds = _dot_tn(u_p, bh_ref[rows, lanes]) + g_m
            s_ref[p] = s_old * wc_ref[pl.ds(n * CHUNK, 1), lanes] + jnp.where(same_head, ds, 0.0)

    y = ybuf_ref[...]
    inv_hd = 1.0 / HD_RWKV
    mean = _dot_exact_lhs(y, bd) * inv_hd
    d = y - mean
    var = _dot_exact_lhs(d * d, bd) * inv_hd
    yn = d * lax.rsqrt(var + GN_EPS) * lw_ref[...] + lb_ref[...]
    y_ref[...] = ((yn + bonus) * g).astype(y_ref.dtype)


def _rwkv(zr, mu, db, du, ib, iu, gu, k_k, k_a, r_k, lw, lb, bd, ltri, lones, batch, seq):
    tb = TB_MIX
    bf_slab = pltpu.VMEM((tb, W_RWKV), BF16)
    f32_slab = pltpu.VMEM((tb, W_RWKV), F32)
    row512 = _const_spec((1, W_RWKV))
    return pl.pallas_call(
        _rwkv_kernel,
        grid=(batch, seq // tb),
        in_specs=[
            pl.BlockSpec((None, tb, ZR_COLS), lambda b, j: (b, j, 0)),
            _const_spec((1, ZR_COLS)),
            row512, _const_spec((LORA_PAD, W_RWKV)),
            row512, _const_spec((LORA_PAD, W_RWKV)),
            _const_spec((LORA_PAD, W_RWKV)),
            row512, row512, row512, row512, row512,
            _const_spec((W_RWKV, W_RWKV)),
            _const_spec((tb, tb)),
            _const_spec((tb, tb)),
        ],
        out_specs=pl.BlockSpec((None, tb, W_RWKV), lambda b, j: (b, j, 0)),
        out_shape=jax.ShapeDtypeStruct((batch, seq, W_RWKV), BF16),
        scratch_shapes=[
            pltpu.VMEM((1, ZR_COLS), F32),
            pltpu.VMEM((W_RWKV // LANES, LANES, LANES), F32),
            bf_slab, bf_slab, bf_slab, bf_slab, bf_slab, bf_slab, bf_slab,
            f32_slab, f32_slab,
        ],
        compiler_params=pltpu.CompilerParams(
            dimension_semantics=("parallel", "arbitrary"), vmem_limit_bytes=VMEM_LIMIT),
        name="rwkv7_mixer",
    )(zr, mu, db, du, ib, iu, gu, k_k, k_a, r_k, lw, lb, bd, ltri, lones)


def _gla_kernel(z_ref, gu_ref, gb_ref, ng_ref, ltri_ref, lones_ref,
                y_ref,
                s_ref, qi_ref, ki_ref, ke_ref, v_ref, dec_ref, obuf_ref):
    tb = z_ref.shape[0]
    n_chunks = tb // CHUNK
    n_pairs = QK_GLA // LANES

    @pl.when(pl.program_id(1) == 0)
    def _():
        s_ref[...] = jnp.zeros_like(s_ref)

    q = z_ref[:, 0:QK_GLA] * (DK_GLA ** -0.5)
    k = z_ref[:, QK_GLA:2 * QK_GLA]
    lo = z_ref[:, 2 * QK_GLA + 2 * W_GLA:ZG_COLS]
    x = _dot(lo.astype(BF16), gu_ref[...]) + gb_ref[...]
    gk = -_softplus(-x) * (1.0 / GK_NORMALIZER)
    b = _dot_exact_rhs(ltri_ref[...], gk)
    b_last = _dot_exact_rhs(lones_ref[...], gk)
    qi_ref[...] = (q * jnp.exp(b)).astype(BF16)
    ki_ref[...] = (k * jnp.exp(-b)).astype(BF16)
    ke_ref[...] = (k * jnp.exp(b_last - b)).astype(BF16)
    v_ref[...] = z_ref[:, 2 * QK_GLA:2 * QK_GLA + W_GLA].astype(BF16)
    dec_ref[...] = jnp.exp(b_last)

    lane_is_h0 = lax.broadcasted_iota(jnp.int32, (CHUNK, LANES), 1) < DK_GLA
    ri = lax.broadcasted_iota(jnp.int32, (PAIR_ROWS, CHUNK), 0)
    ci = lax.broadcasted_iota(jnp.int32, (PAIR_ROWS, CHUNK), 1)
    causal = jnp.where(ri < CHUNK, ri, ri - CHUNK) >= ci

    for n in range(n_chunks):
        rows = pl.ds(n * CHUNK, CHUNK)
        for p in range(n_pairs):
            lanes = pl.ds(p * LANES, LANES)
            q_s = _stack_heads(qi_ref[rows, lanes], lane_is_h0)
            att = jnp.where(causal, _dot_nt(q_s, ki_ref[rows, lanes]), 0.0).astype(BF16)
            v0 = v_ref[rows, pl.ds((2 * p) * DV_GLA, DV_GLA)]
            v1 = v_ref[rows, pl.ds((2 * p + 1) * DV_GLA, DV_GLA)]
            s_old = s_ref[p]
            inter = _dot_nt(q_s, s_old.astype(BF16))
            obuf_ref[rows, pl.ds((2 * p) * DV_GLA, DV_GLA)] = _dot(att[0:CHUNK], v0) + inter[0:CHUNK]
            obuf_ref[rows, pl.ds((2 * p + 1) * DV_GLA, DV_GLA)] = _dot(att[CHUNK:], v1) + inter[CHUNK:]
            ke_s = _stack_heads(ke_ref[rows, lanes], lane_is_h0)
            ds = _dot_tn(jnp.concatenate([v0, v1], axis=0), ke_s)
            s_ref[p] = s_old * dec_ref[pl.ds(n * CHUNK, 1), lanes] + ds

    ng = ng_ref[...]
    for h in range(H_GLA):
        cols = pl.ds(h * DV_GLA, DV_GLA)
        o = obuf_ref[:, cols]
        og = z_ref[:, pl.ds(2 * QK_GLA + W_GLA + h * DV_GLA, DV_GLA)]
        o = o * lax.rsqrt(jnp.mean(o * o, axis=-1, keepdims=True) + GLA_EPS) * ng
        y_ref[:, cols] = (o * (og * _sigmoid(og))).astype(y_ref.dtype)


def _gla(zg, gk_up, gk_bias, norm_g, ltri, lones, batch, seq):
    tb = TB_MIX
    return pl.pallas_call(
        _gla_kernel,
        grid=(batch, seq // tb),
        in_specs=[
            pl.BlockSpec((None, tb, ZG_COLS), lambda b, j: (b, j, 0)),
            _const_spec((GK_PAD, QK_GLA)),
            _const_spec((1, QK_GLA)),
            _const_spec((1, DV_GLA)),
            _const_spec((tb, tb)),
            _const_spec((tb, tb)),
        ],
        out_specs=pl.BlockSpec((None, tb, W_GLA), lambda b, j: (b, j, 0)),
        out_shape=jax.ShapeDtypeStruct((batch, seq, W_GLA), BF16),
        scratch_shapes=[
            pltpu.VMEM((QK_GLA // LANES, DV_GLA, LANES), F32),
            pltpu.VMEM((tb, QK_GLA), BF16),
            pltpu.VMEM((tb, QK_GLA), BF16),
            pltpu.VMEM((tb, QK_GLA), BF16),
            pltpu.VMEM((tb, W_GLA), BF16),
            pltpu.VMEM((tb, QK_GLA), F32),
            pltpu.VMEM((tb, W_GLA), F32),
        ],
        compiler_params=pltpu.CompilerParams(
            dimension_semantics=("parallel", "arbitrary"), vmem_limit_bytes=VMEM_LIMIT),
        name="gla_mixer",
    )(zg, gk_up, gk_bias, norm_g, ltri, lones)


def _ffn_kernel(x_ref, yr_ref, yg_ref, wor_ref, wog_ref, g2_ref, wg_ref, wu_ref, wd_ref, gf_ref, o_ref):
    h = x_ref[...] + _dot(yr_ref[...], wor_ref[...]) + _dot(yg_ref[...], wog_ref[...])
    n = h * lax.rsqrt(jnp.mean(h * h, axis=-1, keepdims=True) + RMS_EPS) * g2_ref[...]
    n = n.astype(BF16)
    gate = _dot(n, wg_ref[...])
    up = _dot(n, wu_ref[...])
    act = (gate * _sigmoid(gate) * up).astype(BF16)
    h = h + _dot(act, wd_ref[...])
    o_ref[...] = h * lax.rsqrt(jnp.mean(h * h, axis=-1, keepdims=True) + RMS_EPS) * gf_ref[...]


def _ffn(x2d, yr, yg, wo_r, wo_g, g2, wg, wu, wd, gf):
    n_tok = x2d.shape[0]
    tm = TM_PROJ
    return pl.pallas_call(
        _ffn_kernel,
        grid=(n_tok // tm,),
        in_specs=[
            pl.BlockSpec((tm, D_MODEL), lambda i: (i, 0)),
            pl.BlockSpec((tm, W_RWKV), lambda i: (i, 0)),
            pl.BlockSpec((tm, W_GLA), lambda i: (i, 0)),
            _const_spec((W_RWKV, D_MODEL)),
            _const_spec((W_GLA, D_MODEL)),
            _const_spec((1, D_MODEL)),
            _const_spec((D_MODEL, D_FF)),
            _const_spec((D_MODEL, D_FF)),
            _const_spec((D_FF, D_MODEL)),
            _const_spec((1, D_MODEL)),
        ],
        out_specs=pl.BlockSpec((tm, D_MODEL), lambda i: (i, 0)),
        out_shape=jax.ShapeDtypeStruct((n_tok, D_MODEL), F32),
        compiler_params=pltpu.CompilerParams(
            dimension_semantics=("parallel",), vmem_limit_bytes=VMEM_LIMIT),
        name="outproj_ffn",
    )(x2d, yr, yg, wo_r, wo_g, g2, wg, wu, wd, gf)


def _pad_cols(w, width):
    return jnp.pad(w, ((0, 0), (0, width - w.shape[1])))


def _block_tri(tb):
    i = jnp.arange(tb)
    same = (i[:, None] // CHUNK) == (i[None, :] // CHUNK)
    ltri = (same & (i[:, None] >= i[None, :])).astype(BF16)
    return ltri, same.astype(BF16)


def kernel(x, rms1_g, w_in, mu_shift, decay_base, decay_up, iclr_base, iclr_up, gate_up, k_k, k_a, r_k,
           lnx_w, lnx_b, gk_up, gk_bias, gla_norm_g, w_out, rms2_g, ffn_gate, ffn_up, ffn_down, final_g):
    batch, seq, _ = x.shape
    depth = w_in.shape[0]
    d_rwkv_in = 3 * W_RWKV + LORA_DECAY + LORA_ICLR + LORA_GATE
    d_gla_in = 2 * QK_GLA + 2 * W_GLA + LORA_GK
    row = lambda t: t.reshape(1, -1).astype(F32)

    ltri, lones = _block_tri(TB_MIX)
    hi = jnp.arange(W_RWKV) // HD_RWKV
    bd = (hi[:, None] == hi[None, :]).astype(BF16)

    assert depth == 1, "single-layer trunk: the final norm is fused into the layer's last kernel"
    l = 0
    h = x.reshape(batch * seq, D_MODEL)

    w_r = _pad_cols(w_in[l][:, :d_rwkv_in], ZR_COLS).astype(BF16)
    w_g = _pad_cols(w_in[l][:, d_rwkv_in:d_rwkv_in + d_gla_in], ZG_COLS).astype(BF16)
    zr, zg = _inproj(h, row(rms1_g[l]), w_r, w_g)

    def lora_rows(w, start):
        return jnp.zeros((LORA_PAD, W_RWKV), F32).at[start:start + w.shape[0]].set(w).astype(BF16)

    du = lora_rows(decay_up[l], 0)
    iu = lora_rows(iclr_up[l], LORA_DECAY)
    gu = lora_rows(gate_up[l], LORA_DECAY + LORA_ICLR)
    mu = _pad_cols(row(mu_shift[l]), ZR_COLS)
    y_rwkv = _rwkv(zr.reshape(batch, seq, ZR_COLS), mu, row(decay_base[l]), du, row(iclr_base[l]), iu, gu,
                   row(k_k[l]), row(k_a[l]), row(r_k[l]), row(lnx_w[l]), row(lnx_b[l]),
                   bd, ltri, lones, batch, seq)

    gku = jnp.zeros((GK_PAD, QK_GLA), F32).at[0:LORA_GK].set(gk_up[l]).astype(BF16)
    y_gla = _gla(zg.reshape(batch, seq, ZG_COLS), gku, row(gk_bias[l]), row(gla_norm_g[l]),
                 ltri, lones, batch, seq)

    out = _ffn(h, y_rwkv.reshape(batch * seq, W_RWKV), y_gla.reshape(batch * seq, W_GLA),
               w_out[l][:W_RWKV].astype(BF16), w_out[l][W_RWKV:].astype(BF16), row(rms2_g[l]),
               ffn_gate[l].astype(BF16), ffn_up[l].astype(BF16), ffn_down[l].astype(BF16), row(final_g))
    return out.reshape(batch, seq, D_MODEL)
```

```python
import functools

import jax
import jax.numpy as jnp
from jax import lax
from jax.experimental import pallas as pl
from jax.experimental.pallas import tpu as pltpu

F32 = jnp.float32
BF16 = jnp.bfloat16

D_MODEL = 1024
W_RWKV = 512
HD_RWKV = 64
LORA_DECAY = 32
LORA_ICLR = 32
LORA_GATE = 96
GN_EPS = 64e-5
W_GLA = 512
DV_GLA = 128
H_GLA = 4
DK_GLA = 64
LORA_GK = 16
GK_NORMALIZER = 16.0
GLA_EPS = 1e-5
D_FF = 2816
RMS_EPS = 1e-6
CHUNK = 64

LANES = 128
PAIR_ROWS = 2 * CHUNK
LORA_PAD = 256
ZR_COLS = 3 * W_RWKV + LORA_PAD
GK_PAD = 128
ZG_COLS = 2 * H_GLA * DK_GLA + 2 * W_GLA + GK_PAD
QK_GLA = H_GLA * DK_GLA

TM_PROJ = 512
TB_MIX = 256
VMEM_LIMIT = 56 * 1024 * 1024


def _dot(a, b):
    return jnp.dot(a, b, preferred_element_type=F32)


def _dot_nt(a, b):
    return lax.dot_general(a, b, (((1,), (1,)), ((), ())), preferred_element_type=F32)


def _dot_tn(a, b):
    return lax.dot_general(a, b, (((0,), (0,)), ((), ())), preferred_element_type=F32)


def _split3(x):
    p0 = x.astype(BF16)
    r = x - p0.astype(F32)
    p1 = r.astype(BF16)
    p2 = (r - p1.astype(F32)).astype(BF16)
    return p0, p1, p2


def _dot_exact_rhs(m01, x):
    p0, p1, p2 = _split3(x)
    return _dot(m01, p0) + _dot(m01, p1) + _dot(m01, p2)


def _dot_exact_lhs(x, m01):
    p0, p1, p2 = _split3(x)
    return _dot(p0, m01) + _dot(p1, m01) + _dot(p2, m01)


def _sigmoid(x):
    return 1.0 / (1.0 + jnp.exp(-x))


def _softplus(x):
    return jnp.maximum(x, 0.0) + jnp.log(1.0 + jnp.exp(-jnp.abs(x)))


def _stack_heads(x, lane_is_h0):
    zero = jnp.zeros_like(x)
    return jnp.concatenate([jnp.where(lane_is_h0, x, zero), jnp.where(lane_is_h0, zero, x)], axis=0)


def _inproj_kernel(x_ref, g_ref, wr_ref, wg_ref, zr_ref, zg_ref):
    x = x_ref[...]
    n = x * lax.rsqrt(jnp.mean(x * x, axis=-1, keepdims=True) + RMS_EPS) * g_ref[...]
    n = n.astype(BF16)
    zr_ref[...] = _dot(n, wr_ref[...])
    zg_ref[...] = _dot(n, wg_ref[...])


def _const_spec(shape):
    return pl.BlockSpec(shape, lambda *_: (0,) * len(shape), pipeline_mode=pl.Buffered(1))


def _inproj(x2d, g, w_r, w_g):
    n_tok = x2d.shape[0]
    return pl.pallas_call(
        _inproj_kernel,
        grid=(n_tok // TM_PROJ,),
        in_specs=[
            pl.BlockSpec((TM_PROJ, D_MODEL), lambda i: (i, 0)),
            _const_spec((1, D_MODEL)),
            _const_spec((D_MODEL, ZR_COLS)),
            _const_spec((D_MODEL, ZG_COLS)),
        ],
        out_specs=[
            pl.BlockSpec((TM_PROJ, ZR_COLS), lambda i: (i, 0)),
            pl.BlockSpec((TM_PROJ, ZG_COLS), lambda i: (i, 0)),
        ],
        out_shape=[
            jax.ShapeDtypeStruct((n_tok, ZR_COLS), F32),
            jax.ShapeDtypeStruct((n_tok, ZG_COLS), F32),
        ],
        compiler_params=pltpu.CompilerParams(
            dimension_semantics=("parallel",), vmem_limit_bytes=VMEM_LIMIT),
        name="inproj",
    )(x2d, g, w_r, w_g)


def _rwkv_kernel(z_ref, mu_ref, db_ref, du_ref, ib_ref, iu_ref, gu_ref, kk_ref, ka_ref, rk_ref,
                 lw_ref, lb_ref, bd_ref, ltri_ref, lones_ref,
                 y_ref,
                 zprev_ref, s_ref, at_ref, rt_ref, bc_ref, kc_ref, bh_ref, kh_ref, v_ref, wc_ref, ybuf_ref):
    tb = z_ref.shape[0]
    n_chunks = tb // CHUNK
    n_pairs = W_RWKV // LANES

    @pl.when(pl.program_id(1) == 0)
    def _():
        zprev_ref[...] = jnp.zeros_like(zprev_ref)
        s_ref[...] = jnp.zeros_like(s_ref)

    z = z_ref[...]
    row = lax.broadcasted_iota(jnp.int32, (tb, 1), 0)
    z_prev = jnp.where(row == 0, zprev_ref[...], pltpu.roll(z, 1, 0))
    zprev_ref[...] = z[tb - 1:tb, :]
    zs = z + (z_prev - z) * mu_ref[...]

    r = zs[:, 0:W_RWKV]
    k = zs[:, W_RWKV:2 * W_RWKV]
    v = zs[:, 2 * W_RWKV:3 * W_RWKV]
    lo = zs[:, 3 * W_RWKV:ZR_COLS]

    bd = bd_ref[...]
    xw = db_ref[...] + _dot(jnp.tanh(lo).astype(BF16), du_ref[...])
    ld = -jnp.exp(-_softplus(-xw) - 0.5)
    a = _sigmoid(ib_ref[...] + _dot(lo.astype(BF16), iu_ref[...]))
    g = _dot(_sigmoid(lo).astype(BF16), gu_ref[...])

    kk = k * kk_ref[...]
    kk = kk * lax.rsqrt(jnp.maximum(_dot_exact_lhs(kk * kk, bd), 1e-24))
    k2 = k * (1.0 + (a - 1.0) * ka_ref[...])
    bonus = _dot_exact_lhs(r * k2 * rk_ref[...], bd) * v
    av = -kk
    bv = kk * a

    cw = _dot_exact_rhs(ltri_ref[...], ld)
    cw_last = _dot_exact_rhs(lones_ref[...], ld)
    e_neg = jnp.exp(-cw)
    e_rem = jnp.exp(cw_last - cw)
    at_ref[...] = (av * jnp.exp(cw - ld)).astype(BF16)
    rt_ref[...] = (r * jnp.exp(cw)).astype(BF16)
    bc_ref[...] = (bv * e_neg).astype(BF16)
    kc_ref[...] = (k2 * e_neg).astype(BF16)
    bh_ref[...] = (bv * e_rem).astype(BF16)
    kh_ref[...] = (k2 * e_rem).astype(BF16)
    v_ref[...] = v.astype(BF16)
    wc_ref[...] = jnp.exp(cw_last)

    lane_is_h0 = lax.broadcasted_iota(jnp.int32, (CHUNK, LANES), 1) < HD_RWKV
    ri = lax.broadcasted_iota(jnp.int32, (PAIR_ROWS, PAIR_ROWS), 0)
    ci = lax.broadcasted_iota(jnp.int32, (PAIR_ROWS, PAIR_ROWS), 1)
    same_head = (ri < CHUNK) == (ci < CHUNK)
    strict_lower = same_head & (ri > ci)
    incl_lower = same_head & (ri >= ci)
    eye = (ri == ci).astype(F32)

    units = [(n, p) for n in range(n_chunks) for p in range(n_pairs)]
    rows_of = lambda n: pl.ds(n * CHUNK, CHUNK)
    lanes_of = lambda p: pl.ds(p * LANES, LANES)
    stacked = lambda ref: [_stack_heads(ref[rows_of(n), lanes_of(p)], lane_is_h0) for n, p in units]
    at_s, rt_s, bc_s, kc_s, v_s = (stacked(ref) for ref in (at_ref, rt_ref, bc_ref, kc_ref, v_ref))

    s1 = [_dot_nt(jnp.concatenate([a_, r_], axis=0), jnp.concatenate([b_, k_], axis=0))
          for a_, r_, b_, k_ in zip(at_s, rt_s, bc_s, kc_s)]
    a_ab = [jnp.where(strict_lower, s[0:PAIR_ROWS, 0:PAIR_ROWS], 0.0) for s in s1]
    a_ak_rk = [jnp.concatenate([jnp.where(strict_lower, s[0:PAIR_ROWS, PAIR_ROWS:], 0.0),
                                jnp.where(incl_lower, s[PAIR_ROWS:, PAIR_ROWS:], 0.0)], axis=0).astype(BF16)
               for s in s1]
    a_rb = [jnp.where(incl_lower, s[PAIR_ROWS:, 0:PAIR_ROWS], 0.0).astype(BF16) for s in s1]
    av_both = [_dot(a_, v_) for a_, v_ in zip(a_ak_rk, v_s)]
    g_m = [_dot_tn(v_ref[rows_of(n), lanes_of(p)], kh_ref[rows_of(n), lanes_of(p)]) for n, p in units]

    t_inv = [eye + a_ for a_ in a_ab]
    a_pow = [a_.astype(BF16) for a_ in a_ab]
    for _ in range(5):
        a_pow = [_dot(a_, a_).astype(BF16) for a_ in a_pow]
        t_inv = [t_ + _dot(t_.astype(BF16), a_) for t_, a_ in zip(t_inv, a_pow)]
    pq = [_dot(t_.astype(BF16), jnp.concatenate([a_, avb[0:PAIR_ROWS].astype(BF16)], axis=1))
          for t_, a_, avb in zip(t_inv, at_s, av_both)]

    state = [s_ref[p] for p in range(n_pairs)]
    for n in range(n_chunks):
        ids = [n * n_pairs + p for p in range(n_pairs)]
        s_bf = [s.astype(BF16) for s in state]
        u = [_dot_nt(pq[i][:, 0:LANES].astype(BF16), s_bf[p]) + pq[i][:, LANES:] for p, i in enumerate(ids)]
        y_s = [_dot_nt(rt_s[i], s_bf[p]) + _dot(a_rb[i], u[p].astype(BF16)) + av_both[i][PAIR_ROWS:]
               for p, i in enumerate(ids)]
        ds = [_dot_tn((u[p][0:CHUNK] + u[p][CHUNK:]).astype(BF16), bh_ref[rows_of(n), lanes_of(p)]) + g_m[i]
              for p, i in enumerate(ids)]
        state = [state[p] * wc_ref[pl.ds(n * CHUNK, 1), lanes_of(p)] + jnp.where(same_head, ds[p], 0.0)
                 for p in range(n_pairs)]
        for p in range(n_pairs):
            ybuf_ref[rows_of(n), lanes_of(p)] = y_s[p][0:CHUNK] + y_s[p][CHUNK:]
    for p in range(n_pairs):
        s_ref[p] = state[p]

    y = ybuf_ref[...]
    inv_hd = 1.0 / HD_RWKV
    mean = _dot_exact_lhs(y, bd) * inv_hd
    d = y - mean
    var = _dot_exact_lhs(d * d, bd) * inv_hd
    yn = d * lax.rsqrt(var + GN_EPS) * lw_ref[...] + lb_ref[...]
    y_ref[...] = ((yn + bonus) * g).astype(y_ref.dtype)


def _rwkv(zr, mu, db, du, ib, iu, gu, k_k, k_a, r_k, lw, lb, bd, ltri, lones, batch, seq):
    tb = TB_MIX
    bf_slab = pltpu.VMEM((tb, W_RWKV), BF16)
    f32_slab = pltpu.VMEM((tb, W_RWKV), F32)
    row512 = _const_spec((1, W_RWKV))
    return pl.pallas_call(
        _rwkv_kernel,
        grid=(batch, seq // tb),
        in_specs=[
            pl.BlockSpec((None, tb, ZR_COLS), lambda b, j: (b, j, 0)),
            _const_spec((1, ZR_COLS)),
            row512, _const_spec((LORA_PAD, W_RWKV)),
            row512, _const_spec((LORA_PAD, W_RWKV)),
            _const_spec((LORA_PAD, W_RWKV)),
            row512, row512, row512, row512, row512,
            _const_spec((W_RWKV, W_RWKV)),
            _const_spec((tb, tb)),
            _const_spec((tb, tb)),
        ],
        out_specs=pl.BlockSpec((None, tb, W_RWKV), lambda b, j: (b, j, 0)),
        out_shape=jax.ShapeDtypeStruct((batch, seq, W_RWKV), BF16),
        scratch_shapes=[
            pltpu.VMEM((1, ZR_COLS), F32),
            pltpu.VMEM((W_RWKV // LANES, LANES, LANES), F32),
            bf_slab, bf_slab, bf_slab, bf_slab, bf_slab, bf_slab, bf_slab,
            f32_slab, f32_slab,
        ],
        compiler_params=pltpu.CompilerParams(
            dimension_semantics=("parallel", "arbitrary"), vmem_limit_bytes=VMEM_LIMIT),
        name="rwkv7_mixer",
    )(zr, mu, db, du, ib, iu, gu, k_k, k_a, r_k, lw, lb, bd, ltri, lones)


def _gla_kernel(z_ref, gu_ref, gb_ref, ng_ref, ltri_ref, lones_ref,
                y_ref,
                s_ref, qi_ref, ki_ref, ke_ref, v_ref, dec_ref, obuf_ref):
    tb = z_ref.shape[0]
    n_chunks = tb // CHUNK
    n_pairs = QK_GLA // LANES

    @pl.when(pl.program_id(1) == 0)
    def _():
        s_ref[...] = jnp.zeros_like(s_ref)

    q = z_ref[:, 0:QK_GLA] * (DK_GLA ** -0.5)
    k = z_ref[:, QK_GLA:2 * QK_GLA]
    lo = z_ref[:, 2 * QK_GLA + 2 * W_GLA:ZG_COLS]
    x = _dot(lo.astype(BF16), gu_ref[...]) + gb_ref[...]
    gk = -_softplus(-x) * (1.0 / GK_NORMALIZER)
    b = _dot_exact_rhs(ltri_ref[...], gk)
    b_last = _dot_exact_rhs(lones_ref[...], gk)
    qi_ref[...] = (q * jnp.exp(b)).astype(BF16)
    ki_ref[...] = (k * jnp.exp(-b)).astype(BF16)
    ke_ref[...] = (k * jnp.exp(b_last - b)).astype(BF16)
    v_ref[...] = z_ref[:, 2 * QK_GLA:2 * QK_GLA + W_GLA].astype(BF16)
    dec_ref[...] = jnp.exp(b_last)

    lane_is_h0 = lax.broadcasted_iota(jnp.int32, (CHUNK, LANES), 1) < DK_GLA
    ri = lax.broadcasted_iota(jnp.int32, (PAIR_ROWS, CHUNK), 0)
    ci = lax.broadcasted_iota(jnp.int32, (PAIR_ROWS, CHUNK), 1)
    causal = jnp.where(ri < CHUNK, ri, ri - CHUNK) >= ci

    for n in range(n_chunks):
        rows = pl.ds(n * CHUNK, CHUNK)
        for p in range(n_pairs):
            lanes = pl.ds(p * LANES, LANES)
            q_s = _stack_heads(qi_ref[rows, lanes], lane_is_h0)
            att = jnp.where(causal, _dot_nt(q_s, ki_ref[rows, lanes]), 0.0).astype(BF16)
            v0 = v_ref[rows, pl.ds((2 * p) * DV_GLA, DV_GLA)]
            v1 = v_ref[rows, pl.ds((2 * p + 1) * DV_GLA, DV_GLA)]
            s_old = s_ref[p]
            inter = _dot_nt(q_s, s_old.astype(BF16))
            obuf_ref[rows, pl.ds((2 * p) * DV_GLA, DV_GLA)] = _dot(att[0:CHUNK], v0) + inter[0:CHUNK]
            obuf_ref[rows, pl.ds((2 * p + 1) * DV_GLA, DV_GLA)] = _dot(att[CHUNK:], v1) + inter[CHUNK:]
            ke_s = _stack_heads(ke_ref[rows, lanes], lane_is_h0)
            ds = _dot_tn(jnp.concatenate([v0, v1], axis=0), ke_s)
            s_ref[p] = s_old * dec_ref[pl.ds(n * CHUNK, 1), lanes] + ds

    ng = ng_ref[...]
    for h in range(H_GLA):
        cols = pl.ds(h * DV_GLA, DV_GLA)
        o = obuf_ref[:, cols]
        og = z_ref[:, pl.ds(2 * QK_GLA + W_GLA + h * DV_GLA, DV_GLA)]
        o = o * lax.rsqrt(jnp.mean(o * o, axis=-1, keepdims=True) + GLA_EPS) * ng
        y_ref[:, cols] = (o * (og * _sigmoid(og))).astype(y_ref.dtype)


def _gla(zg, gk_up, gk_bias, norm_g, ltri, lones, batch, seq):
    tb = TB_MIX
    return pl.pallas_call(
        _gla_kernel,
        grid=(batch, seq // tb),
        in_specs=[
            pl.BlockSpec((None, tb, ZG_COLS), lambda b, j: (b, j, 0)),
            _const_spec((GK_PAD, QK_GLA)),
            _const_spec((1, QK_GLA)),
            _const_spec((1, DV_GLA)),
            _const_spec((tb, tb)),
            _const_spec((tb, tb)),
        ],
        out_specs=pl.BlockSpec((None, tb, W_GLA), lambda b, j: (b, j, 0)),
        out_shape=jax.ShapeDtypeStruct((batch, seq, W_GLA), BF16),
        scratch_shapes=[
            pltpu.VMEM((QK_GLA // LANES, DV_GLA, LANES), F32),
            pltpu.VMEM((tb, QK_GLA), BF16),
            pltpu.VMEM((tb, QK_GLA), BF16),
            pltpu.VMEM((tb, QK_GLA), BF16),
            pltpu.VMEM((tb, W_GLA), BF16),
            pltpu.VMEM((tb, QK_GLA), F32),
            pltpu.VMEM((tb, W_GLA), F32),
        ],
        compiler_params=pltpu.CompilerParams(
            dimension_semantics=("parallel", "arbitrary"), vmem_limit_bytes=VMEM_LIMIT),
        name="gla_mixer",
    )(zg, gk_up, gk_bias, norm_g, ltri, lones)


def _ffn_kernel(x_ref, yr_ref, yg_ref, wor_ref, wog_ref, g2_ref, wg_ref, wu_ref, wd_ref, gf_ref, o_ref):
    h = x_ref[...] + _dot(yr_ref[...], wor_ref[...]) + _dot(yg_ref[...], wog_ref[...])
    n = h * lax.rsqrt(jnp.mean(h * h, axis=-1, keepdims=True) + RMS_EPS) * g2_ref[...]
    n = n.astype(BF16)
    gate = _dot(n, wg_ref[...])
    up = _dot(n, wu_ref[...])
    act = (gate * _sigmoid(gate) * up).astype(BF16)
    h = h + _dot(act, wd_ref[...])
    o_ref[...] = h * lax.rsqrt(jnp.mean(h * h, axis=-1, keepdims=True) + RMS_EPS) * gf_ref[...]


def _ffn(x2d, yr, yg, wo_r, wo_g, g2, wg, wu, wd, gf):
    n_tok = x2d.shape[0]
    tm = TM_PROJ
    return pl.pallas_call(
        _ffn_kernel,
        grid=(n_tok // tm,),
        in_specs=[
            pl.BlockSpec((tm, D_MODEL), lambda i: (i, 0)),
            pl.BlockSpec((tm, W_RWKV), lambda i: (i, 0)),
            pl.BlockSpec((tm, W_GLA), lambda i: (i, 0)),
            _const_spec((W_RWKV, D_MODEL)),
            _const_spec((W_GLA, D_MODEL)),
            _const_spec((1, D_MODEL)),
            _const_spec((D_MODEL, D_FF)),
            _const_spec((D_MODEL, D_FF)),
            _const_spec((D_FF, D_MODEL)),
            _const_spec((1, D_MODEL)),
        ],
        out_specs=pl.BlockSpec((tm, D_MODEL), lambda i: (i, 0)),
        out_shape=jax.ShapeDtypeStruct((n_tok, D_MODEL), F32),
        compiler_params=pltpu.CompilerParams(
            dimension_semantics=("parallel",), vmem_limit_bytes=VMEM_LIMIT),
        name="outproj_ffn",
    )(x2d, yr, yg, wo_r, wo_g, g2, wg, wu, wd, gf)


def _pad_cols(w, width):
    return jnp.pad(w, ((0, 0), (0, width - w.shape[1])))


def _block_tri(tb):
    i = jnp.arange(tb)
    same = (i[:, None] // CHUNK) == (i[None, :] // CHUNK)
    ltri = (same & (i[:, None] >= i[None, :])).astype(BF16)
    return ltri, same.astype(BF16)


def kernel(x, rms1_g, w_in, mu_shift, decay_base, decay_up, iclr_base, iclr_up, gate_up, k_k, k_a, r_k,
           lnx_w, lnx_b, gk_up, gk_bias, gla_norm_g, w_out, rms2_g, ffn_gate, ffn_up, ffn_down, final_g):
    batch, seq, _ = x.shape
    depth = w_in.shape[0]
    d_rwkv_in = 3 * W_RWKV + LORA_DECAY + LORA_ICLR + LORA_GATE
    d_gla_in = 2 * QK_GLA + 2 * W_GLA + LORA_GK
    row = lambda t: t.reshape(1, -1).astype(F32)

    ltri, lones = _block_tri(TB_MIX)
    hi = jnp.arange(W_RWKV) // HD_RWKV
    bd = (hi[:, None] == hi[None, :]).astype(BF16)

    assert depth == 1, "single-layer trunk: the final norm is fused into the layer's last kernel"
    l = 0
    h = x.reshape(batch * seq, D_MODEL)

    w_r = _pad_cols(w_in[l][:, :d_rwkv_in], ZR_COLS).astype(BF16)
    w_g = _pad_cols(w_in[l][:, d_rwkv_in:d_rwkv_in + d_gla_in], ZG_COLS).astype(BF16)
    zr, zg = _inproj(h, row(rms1_g[l]), w_r, w_g)

    def lora_rows(w, start):
        return jnp.zeros((LORA_PAD, W_RWKV), F32).at[start:start + w.shape[0]].set(w).astype(BF16)

    du = lora_rows(decay_up[l], 0)
    iu = lora_rows(iclr_up[l], LORA_DECAY)
    gu = lora_rows(gate_up[l], LORA_DECAY + LORA_ICLR)
    mu = _pad_cols(row(mu_shift[l]), ZR_COLS)
    y_rwkv = _rwkv(zr.reshape(batch, seq, ZR_COLS), mu, row(decay_base[l]), du, row(iclr_base[l]), iu, gu,
                   row(k_k[l]), row(k_a[l]), row(r_k[l]), row(lnx_w[l]), row(lnx_b[l]),
                   bd, ltri, lones, batch, seq)

    gku = jnp.zeros((GK_PAD, QK_GLA), F32).at[0:LORA_GK].set(gk_up[l]).astype(BF16)
    y_gla = _gla(zg.reshape(batch, seq, ZG_COLS), gku, row(gk_bias[l]), row(gla_norm_g[l]),
                 ltri, lones, batch, seq)

    out = _ffn(h, y_rwkv.reshape(batch * seq, W_RWKV), y_gla.reshape(batch * seq, W_GLA),
               w_out[l][:W_RWKV].astype(BF16), w_out[l][W_RWKV:].astype(BF16), row(rms2_g[l]),
               ffn_gate[l].astype(BF16), ffn_up[l].astype(BF16), ffn_down[l].astype(BF16), row(final_g))
    return out.reshape(batch, seq, D_MODEL)
```

```python
import math

import jax
import jax.numpy as jnp
from jax import lax
from jax.experimental import pallas as pl
from jax.experimental.pallas import tpu as pltpu

F32 = jnp.float32
BF16 = jnp.bfloat16

D_MODEL = 1024
W_RWKV = 512
HD_RWKV = 64
LORA_DECAY = 32
LORA_ICLR = 32
LORA_GATE = 96
GN_EPS = 64e-5
W_GLA = 512
DV_GLA = 128
H_GLA = 4
DK_GLA = 64
LORA_GK = 16
GK_NORMALIZER = 16.0
GLA_EPS = 1e-5
D_FF = 2816
RMS_EPS = 1e-6
CHUNK = 64

LANES = 128
PAIR_ROWS = 2 * CHUNK
LORA_PAD = 256
ZR_COLS = 3 * W_RWKV + LORA_PAD
GK_PAD = 128
ZG_COLS = 2 * H_GLA * DK_GLA + 2 * W_GLA + GK_PAD
QK_GLA = H_GLA * DK_GLA

TM_PROJ = 512
TB_MIX = 256
VMEM_LIMIT = 56 * 1024 * 1024


def _dot(a, b):
    return jnp.dot(a, b, preferred_element_type=F32)


def _dot_nt(a, b):
    return lax.dot_general(a, b, (((1,), (1,)), ((), ())), preferred_element_type=F32)


def _dot_tn(a, b):
    return lax.dot_general(a, b, (((0,), (0,)), ((), ())), preferred_element_type=F32)


def _split3(x):
    p0 = x.astype(BF16)
    r = x - p0.astype(F32)
    p1 = r.astype(BF16)
    p2 = (r - p1.astype(F32)).astype(BF16)
    return p0, p1, p2


def _dot_exact_rhs(m01, x):
    p0, p1, p2 = _split3(x)
    return _dot(m01, p0) + _dot(m01, p1) + _dot(m01, p2)


def _head_sums(x, ones_blocks):
    hi = x.astype(BF16)
    lo = (x - hi.astype(F32)).astype(BF16)
    width = ones_blocks.shape[0]
    parts = [_dot(hi[:, c:c + width], ones_blocks) + _dot(lo[:, c:c + width], ones_blocks)
             for c in range(0, x.shape[1], width)]
    return jnp.concatenate(parts, axis=1)


def _sigmoid(x):
    return 1.0 / (1.0 + jnp.exp(-x))


def _softplus(x):
    return jnp.maximum(x, 0.0) + jnp.log(1.0 + jnp.exp(-jnp.abs(x)))


def _stack_heads(x, lane_is_h0):
    zero = jnp.zeros_like(x)
    return jnp.concatenate([jnp.where(lane_is_h0, x, zero), jnp.where(lane_is_h0, zero, x)], axis=0)


def _inproj_kernel(x_ref, g_ref, wr_ref, wg_ref, zr_ref, zg_ref):
    x = x_ref[...]
    n = x * lax.rsqrt(jnp.mean(x * x, axis=-1, keepdims=True) + RMS_EPS) * g_ref[...]
    n = n.astype(BF16)
    zr_ref[...] = _dot(n, wr_ref[...])
    zg_ref[...] = _dot(n, wg_ref[...])


def _const_spec(shape):
    return pl.BlockSpec(shape, lambda *_: (0,) * len(shape), pipeline_mode=pl.Buffered(1))


def _inproj(x2d, g, w_r, w_g):
    n_tok = x2d.shape[0]
    return pl.pallas_call(
        _inproj_kernel,
        grid=(n_tok // TM_PROJ,),
        in_specs=[
            pl.BlockSpec((TM_PROJ, D_MODEL), lambda i: (i, 0)),
            _const_spec((1, D_MODEL)),
            _const_spec((D_MODEL, ZR_COLS)),
            _const_spec((D_MODEL, ZG_COLS)),
        ],
        out_specs=[
            pl.BlockSpec((TM_PROJ, ZR_COLS), lambda i: (i, 0)),
            pl.BlockSpec((TM_PROJ, ZG_COLS), lambda i: (i, 0)),
        ],
        out_shape=[
            jax.ShapeDtypeStruct((n_tok, ZR_COLS), F32),
            jax.ShapeDtypeStruct((n_tok, ZG_COLS), F32),
        ],
        compiler_params=pltpu.CompilerParams(
            dimension_semantics=("parallel",), vmem_limit_bytes=VMEM_LIMIT),
        name="inproj",
    )(x2d, g, w_r, w_g)


def _rwkv_kernel(z_ref, mu_ref, db_ref, du_ref, ib_ref, iu_ref, gu_ref, kk_ref, ka_ref, rk_ref,
                 lw_ref, lb_ref, ones_ref, ltri_ref,
                 y_ref,
                 zprev_ref, s_ref, at_ref, rt_ref, bc_ref, kc_ref, bh_ref, kh_ref, v_ref, ybuf_ref):
    tb = z_ref.shape[0]
    n_chunks = tb // CHUNK
    n_pairs = W_RWKV // LANES

    @pl.when(pl.program_id(1) == 0)
    def _():
        zprev_ref[...] = jnp.zeros_like(zprev_ref)
        s_ref[...] = jnp.zeros_like(s_ref)

    z = z_ref[...]
    row = lax.broadcasted_iota(jnp.int32, (tb, 1), 0)
    z_prev = jnp.where(row == 0, zprev_ref[...], pltpu.roll(z, 1, 0))
    zprev_ref[...] = z[tb - 1:tb, :]
    zs = z + (z_prev - z) * mu_ref[...]

    r = zs[:, 0:W_RWKV]
    k = zs[:, W_RWKV:2 * W_RWKV]
    v = zs[:, 2 * W_RWKV:3 * W_RWKV]
    lo = zs[:, 3 * W_RWKV:ZR_COLS]

    ones_blocks = ones_ref[...]
    xw = db_ref[...] + _dot(jnp.tanh(lo).astype(BF16), du_ref[...])
    ld = _sigmoid(xw) * (-math.exp(-0.5))
    a = _sigmoid(ib_ref[...] + _dot(lo.astype(BF16), iu_ref[...]))
    g = _dot(_sigmoid(lo).astype(BF16), gu_ref[...])

    kk = k * kk_ref[...]
    kk = kk * lax.rsqrt(jnp.maximum(_head_sums(kk * kk, ones_blocks), 1e-24))
    k2 = k * (1.0 + (a - 1.0) * ka_ref[...])
    bonus = _head_sums(r * k2 * rk_ref[...], ones_blocks) * v
    av = -kk
    bv = kk * a

    cw = _dot_exact_rhs(ltri_ref[...], ld)
    e_cw = jnp.exp(cw)
    e_neg = jnp.exp(-cw)
    at_ref[...] = (av * jnp.exp(cw - ld)).astype(BF16)
    rt_ref[...] = (r * e_cw).astype(BF16)
    bnc = bv * e_neg
    knc = k2 * e_neg
    bc_ref[...] = bnc.astype(BF16)
    kc_ref[...] = knc.astype(BF16)
    v_ref[...] = v.astype(BF16)
    chunk_decay = []
    for n in range(n_chunks):
        last = (n + 1) * CHUNK - 1
        w_c = e_cw[last:last + 1, :]
        chunk_decay.append(w_c)
        rows = slice(n * CHUNK, (n + 1) * CHUNK)
        bh_ref[rows, :] = (bnc[rows] * w_c).astype(BF16)
        kh_ref[rows, :] = (knc[rows] * w_c).astype(BF16)

    lane_is_h0 = lax.broadcasted_iota(jnp.int32, (CHUNK, LANES), 1) < HD_RWKV
    ti = lax.broadcasted_iota(jnp.int32, (CHUNK, LANES), 0)
    tj = lax.broadcasted_iota(jnp.int32, (CHUNK, LANES), 1) % CHUNK
    strict_lower = ti > tj
    incl_lower = ti >= tj
    eye = (ti == tj).astype(F32)
    ri = lax.broadcasted_iota(jnp.int32, (PAIR_ROWS, LANES), 0)
    ci = lax.broadcasted_iota(jnp.int32, (PAIR_ROWS, LANES), 1)
    same_head = (ri < HD_RWKV) == (ci < HD_RWKV)
    stack = lambda x: _stack_heads(x, lane_is_h0)
    bf = lambda x: x.astype(BF16)

    units = [(n, p) for n in range(n_chunks) for p in range(n_pairs)]
    rows_of = lambda n: pl.ds(n * CHUNK, CHUNK)
    lanes_of = lambda p: pl.ds(p * LANES, LANES)
    load = lambda ref: [ref[rows_of(n), lanes_of(p)] for n, p in units]
    at_l, rt_l, bc_l, kc_l, v_l, bh_l, kh_l = (
        load(ref) for ref in (at_ref, rt_ref, bc_ref, kc_ref, v_ref, bh_ref, kh_ref))

    x4 = [_dot_nt(jnp.concatenate([a_, r_], axis=0), jnp.concatenate([stack(b_), stack(k_)], axis=0))
          for a_, r_, b_, k_ in zip(at_l, rt_l, bc_l, kc_l)]
    a_ab = [jnp.where(strict_lower, x[0:CHUNK, 0:LANES], 0.0) for x in x4]
    a_k = [bf(jnp.concatenate([jnp.where(strict_lower, x[0:CHUNK, LANES:], 0.0),
                               jnp.where(incl_lower, x[CHUNK:, LANES:], 0.0)], axis=0)) for x in x4]
    a_rb = [bf(jnp.where(incl_lower, x[CHUNK:, 0:LANES], 0.0)) for x in x4]
    av_both = [_dot(a_, stack(v_)) for a_, v_ in zip(a_k, v_l)]

    t_inv = [eye + a_ for a_ in a_ab]
    a_pow = [bf(a_) for a_ in a_ab]
    a_pow = [bf(_dot(a_, stack(a_))) for a_ in a_pow]
    for _ in range(4):
        x2 = [_dot(a_, jnp.concatenate([stack(a_), stack(bf(t_))], axis=1)) for a_, t_ in zip(a_pow, t_inv)]
        t_inv = [t_ + x[:, LANES:] for t_, x in zip(t_inv, x2)]
        a_pow = [bf(x[:, 0:LANES]) for x in x2]
    t_inv = [t_ + _dot(a_, stack(bf(t_))) for t_, a_ in zip(t_inv, a_pow)]
    pq = [_dot(bf(t_), jnp.concatenate([stack(a_), stack(bf(avb[0:CHUNK]))], axis=1))
          for t_, a_, avb in zip(t_inv, at_l, av_both)]
    p_m = [bf(x[:, 0:LANES]) for x in pq]
    q_m = [x[:, LANES:] for x in pq]
    m_p = [bf(jnp.where(same_head, _dot_tn(p_, b_), 0.0)) for p_, b_ in zip(p_m, bh_l)]
    n_p = [jnp.where(same_head,
                     _dot_tn(jnp.concatenate([bf(q_), v_], axis=0), jnp.concatenate([b_, k_], axis=0)), 0.0)
           for q_, v_, b_, k_ in zip(q_m, v_l, bh_l, kh_l)]

    state = [s_ref[p] for p in range(n_pairs)]
    for n in range(n_chunks):
        ids = [n * n_pairs + p for p in range(n_pairs)]
        s_bf = [bf(s) for s in state]
        state = [s * chunk_decay[n][:, p * LANES:(p + 1) * LANES] + _dot(sb, m_p[i]) + n_p[i]
                 for p, (i, s, sb) in enumerate(zip(ids, state, s_bf))]
        uy = [_dot_nt(jnp.concatenate([p_m[i], rt_l[i]], axis=0), sb) for i, sb in zip(ids, s_bf)]
        u = [x[0:CHUNK] + q_m[i] for i, x in zip(ids, uy)]
        y_c = [x[CHUNK:] + _dot(a_rb[i], stack(bf(u_))) + av_both[i][CHUNK:] for i, x, u_ in zip(ids, uy, u)]
        for p in range(n_pairs):
            ybuf_ref[rows_of(n), lanes_of(p)] = y_c[p]
    for p in range(n_pairs):
        s_ref[p] = state[p]

    y = ybuf_ref[...]
    inv_hd = 1.0 / HD_RWKV
    mean = _head_sums(y, ones_blocks) * inv_hd
    d = y - mean
    var = _head_sums(d * d, ones_blocks) * inv_hd
    yn = d * lax.rsqrt(var + GN_EPS) * lw_ref[...] + lb_ref[...]
    y_ref[...] = ((yn + bonus) * g).astype(y_ref.dtype)


def _rwkv(zr, mu, db, du, ib, iu, gu, k_k, k_a, r_k, lw, lb, ones_blocks, ltri, batch, seq):
    tb = TB_MIX
    bf_slab = pltpu.VMEM((tb, W_RWKV), BF16)
    row512 = _const_spec((1, W_RWKV))
    return pl.pallas_call(
        _rwkv_kernel,
        grid=(batch, seq // tb),
        in_specs=[
            pl.BlockSpec((None, tb, ZR_COLS), lambda b, j: (b, j, 0)),
            _const_spec((1, ZR_COLS)),
            row512, _const_spec((LORA_PAD, W_RWKV)),
            row512, _const_spec((LORA_PAD, W_RWKV)),
            _const_spec((LORA_PAD, W_RWKV)),
            row512, row512, row512, row512, row512,
            _const_spec((2 * LANES, 2 * LANES)),
            _const_spec((tb, tb)),
        ],
        out_specs=pl.BlockSpec((None, tb, W_RWKV), lambda b, j: (b, j, 0)),
        out_shape=jax.ShapeDtypeStruct((batch, seq, W_RWKV), BF16),
        scratch_shapes=[
            pltpu.VMEM((1, ZR_COLS), F32),
            pltpu.VMEM((W_RWKV // LANES, LANES, LANES), F32),
            bf_slab, bf_slab, bf_slab, bf_slab, bf_slab, bf_slab, bf_slab,
            pltpu.VMEM((tb, W_RWKV), F32),
        ],
        compiler_params=pltpu.CompilerParams(
            dimension_semantics=("parallel", "arbitrary"), vmem_limit_bytes=VMEM_LIMIT),
        name="rwkv7_mixer",
    )(zr, mu, db, du, ib, iu, gu, k_k, k_a, r_k, lw, lb, ones_blocks, ltri)


def _gla_kernel(z_ref, gu_ref, gb_ref, ng_ref, ltri_ref, lones_ref,
                y_ref,
                s_ref, qi_ref, ki_ref, ke_ref, v_ref, dec_ref, obuf_ref):
    tb = z_ref.shape[0]
    n_chunks = tb // CHUNK
    n_pairs = QK_GLA // LANES

    @pl.when(pl.program_id(1) == 0)
    def _():
        s_ref[...] = jnp.zeros_like(s_ref)

    q = z_ref[:, 0:QK_GLA] * (DK_GLA ** -0.5)
    k = z_ref[:, QK_GLA:2 * QK_GLA]
    lo = z_ref[:, 2 * QK_GLA + 2 * W_GLA:ZG_COLS]
    x = _dot(lo.astype(BF16), gu_ref[...]) + gb_ref[...]
    gk = -_softplus(-x) * (1.0 / GK_NORMALIZER)
    b = _dot_exact_rhs(ltri_ref[...], gk)
    b_last = _dot_exact_rhs(lones_ref[...], gk)
    qi_ref[...] = (q * jnp.exp(b)).astype(BF16)
    ki_ref[...] = (k * jnp.exp(-b)).astype(BF16)
    ke_ref[...] = (k * jnp.exp(b_last - b)).astype(BF16)
    v_ref[...] = z_ref[:, 2 * QK_GLA:2 * QK_GLA + W_GLA].astype(BF16)
    dec_ref[...] = jnp.exp(b_last)

    lane_is_h0 = lax.broadcasted_iota(jnp.int32, (CHUNK, LANES), 1) < DK_GLA
    ri = lax.broadcasted_iota(jnp.int32, (PAIR_ROWS, CHUNK), 0)
    ci = lax.broadcasted_iota(jnp.int32, (PAIR_ROWS, CHUNK), 1)
    causal = jnp.where(ri < CHUNK, ri, ri - CHUNK) >= ci

    for n in range(n_chunks):
        rows = pl.ds(n * CHUNK, CHUNK)
        for p in range(n_pairs):
            lanes = pl.ds(p * LANES, LANES)
            q_s = _stack_heads(qi_ref[rows, lanes], lane_is_h0)
            att = jnp.where(causal, _dot_nt(q_s, ki_ref[rows, lanes]), 0.0).astype(BF16)
            v0 = v_ref[rows, pl.ds((2 * p) * DV_GLA, DV_GLA)]
            v1 = v_ref[rows, pl.ds((2 * p + 1) * DV_GLA, DV_GLA)]
            s_old = s_ref[p]
            inter = _dot_nt(q_s, s_old.astype(BF16))
            obuf_ref[rows, pl.ds((2 * p) * DV_GLA, DV_GLA)] = _dot(att[0:CHUNK], v0) + inter[0:CHUNK]
            obuf_ref[rows, pl.ds((2 * p + 1) * DV_GLA, DV_GLA)] = _dot(att[CHUNK:], v1) + inter[CHUNK:]
            ke_s = _stack_heads(ke_ref[rows, lanes], lane_is_h0)
            ds = _dot_tn(jnp.concatenate([v0, v1], axis=0), ke_s)
            s_ref[p] = s_old * dec_ref[pl.ds(n * CHUNK, 1), lanes] + ds

    ng = ng_ref[...]
    for h in range(H_GLA):
        cols = pl.ds(h * DV_GLA, DV_GLA)
        o = obuf_ref[:, cols]
        og = z_ref[:, pl.ds(2 * QK_GLA + W_GLA + h * DV_GLA, DV_GLA)]
        o = o * lax.rsqrt(jnp.mean(o * o, axis=-1, keepdims=True) + GLA_EPS) * ng
        y_ref[:, cols] = (o * (og * _sigmoid(og))).astype(y_ref.dtype)


def _gla(zg, gk_up, gk_bias, norm_g, ltri, lones, batch, seq):
    tb = TB_MIX
    return pl.pallas_call(
        _gla_kernel,
        grid=(batch, seq // tb),
        in_specs=[
            pl.BlockSpec((None, tb, ZG_COLS), lambda b, j: (b, j, 0)),
            _const_spec((GK_PAD, QK_GLA)),
            _const_spec((1, QK_GLA)),
            _const_spec((1, DV_GLA)),
            _const_spec((tb, tb)),
            _const_spec((tb, tb)),
        ],
        out_specs=pl.BlockSpec((None, tb, W_GLA), lambda b, j: (b, j, 0)),
        out_shape=jax.ShapeDtypeStruct((batch, seq, W_GLA), BF16),
        scratch_shapes=[
            pltpu.VMEM((QK_GLA // LANES, DV_GLA, LANES), F32),
            pltpu.VMEM((tb, QK_GLA), BF16),
            pltpu.VMEM((tb, QK_GLA), BF16),
            pltpu.VMEM((tb, QK_GLA), BF16),
            pltpu.VMEM((tb, W_GLA), BF16),
            pltpu.VMEM((tb, QK_GLA), F32),
            pltpu.VMEM((tb, W_GLA), F32),
        ],
        compiler_params=pltpu.CompilerParams(
            dimension_semantics=("parallel", "arbitrary"), vmem_limit_bytes=VMEM_LIMIT),
        name="gla_mixer",
    )(zg, gk_up, gk_bias, norm_g, ltri, lones)


def _ffn_kernel(x_ref, yr_ref, yg_ref, wor_ref, wog_ref, g2_ref, wg_ref, wu_ref, wd_ref, gf_ref, o_ref):
    h = x_ref[...] + _dot(yr_ref[...], wor_ref[...]) + _dot(yg_ref[...], wog_ref[...])
    n = h * lax.rsqrt(jnp.mean(h * h, axis=-1, keepdims=True) + RMS_EPS) * g2_ref[...]
    n = n.astype(BF16)
    gate = _dot(n, wg_ref[...])
    up = _dot(n, wu_ref[...])
    act = (gate * _sigmoid(gate) * up).astype(BF16)
    h = h + _dot(act, wd_ref[...])
    o_ref[...] = h * lax.rsqrt(jnp.mean(h * h, axis=-1, keepdims=True) + RMS_EPS) * gf_ref[...]


def _ffn(x2d, yr, yg, wo_r, wo_g, g2, wg, wu, wd, gf):
    n_tok = x2d.shape[0]
    tm = TM_PROJ
    return pl.pallas_call(
        _ffn_kernel,
        grid=(n_tok // tm,),
        in_specs=[
            pl.BlockSpec((tm, D_MODEL), lambda i: (i, 0)),
            pl.BlockSpec((tm, W_RWKV), lambda i: (i, 0)),
            pl.BlockSpec((tm, W_GLA), lambda i: (i, 0)),
            _const_spec((W_RWKV, D_MODEL)),
            _const_spec((W_GLA, D_MODEL)),
            _const_spec((1, D_MODEL)),
            _const_spec((D_MODEL, D_FF)),
            _const_spec((D_MODEL, D_FF)),
            _const_spec((D_FF, D_MODEL)),
            _const_spec((1, D_MODEL)),
        ],
        out_specs=pl.BlockSpec((tm, D_MODEL), lambda i: (i, 0)),
        out_shape=jax.ShapeDtypeStruct((n_tok, D_MODEL), F32),
        compiler_params=pltpu.CompilerParams(
            dimension_semantics=("parallel",), vmem_limit_bytes=VMEM_LIMIT),
        name="outproj_ffn",
    )(x2d, yr, yg, wo_r, wo_g, g2, wg, wu, wd, gf)


def _pad_cols(w, width):
    return jnp.pad(w, ((0, 0), (0, width - w.shape[1])))


def _block_tri(tb):
    i = jnp.arange(tb)
    same = (i[:, None] // CHUNK) == (i[None, :] // CHUNK)
    ltri = (same & (i[:, None] >= i[None, :])).astype(BF16)
    return ltri, same.astype(BF16)


def kernel(x, rms1_g, w_in, mu_shift, decay_base, decay_up, iclr_base, iclr_up, gate_up, k_k, k_a, r_k,
           lnx_w, lnx_b, gk_up, gk_bias, gla_norm_g, w_out, rms2_g, ffn_gate, ffn_up, ffn_down, final_g):
    batch, seq, _ = x.shape
    depth = w_in.shape[0]
    d_rwkv_in = 3 * W_RWKV + LORA_DECAY + LORA_ICLR + LORA_GATE
    d_gla_in = 2 * QK_GLA + 2 * W_GLA + LORA_GK
    row = lambda t: t.reshape(1, -1).astype(F32)

    ltri, lones = _block_tri(TB_MIX)
    hi = jnp.arange(2 * LANES) // HD_RWKV
    ones_blocks = (hi[:, None] == hi[None, :]).astype(BF16)

    assert depth == 1, "single-layer trunk: the final norm is fused into the layer's last kernel"
    l = 0
    h = x.reshape(batch * seq, D_MODEL)

    w_r = _pad_cols(w_in[l][:, :d_rwkv_in], ZR_COLS).astype(BF16)
    w_g = _pad_cols(w_in[l][:, d_rwkv_in:d_rwkv_in + d_gla_in], ZG_COLS).astype(BF16)
    zr, zg = _inproj(h, row(rms1_g[l]), w_r, w_g)

    def lora_rows(w, start):
        return jnp.zeros((LORA_PAD, W_RWKV), F32).at[start:start + w.shape[0]].set(w).astype(BF16)

    du = lora_rows(decay_up[l], 0)
    iu = lora_rows(iclr_up[l], LORA_DECAY)
    gu = lora_rows(gate_up[l], LORA_DECAY + LORA_ICLR)
    mu = _pad_cols(row(mu_shift[l]), ZR_COLS)
    y_rwkv = _rwkv(zr.reshape(batch, seq, ZR_COLS), mu, row(decay_base[l]), du, row(iclr_base[l]), iu, gu,
                   row(k_k[l]), row(k_a[l]), row(r_k[l]), row(lnx_w[l]), row(lnx_b[l]),
                   ones_blocks, ltri, batch, seq)

    gku = jnp.zeros((GK_PAD, QK_GLA), F32).at[0:LORA_GK].set(gk_up[l]).astype(BF16)
    y_gla = _gla(zg.reshape(batch, seq, ZG_COLS), gku, row(gk_bias[l]), row(gla_norm_g[l]),
                 ltri, lones, batch, seq)

    out = _ffn(h, y_rwkv.reshape(batch * seq, W_RWKV), y_gla.reshape(batch * seq, W_GLA),
               w_out[l][:W_RWKV].astype(BF16), w_out[l][W_RWKV:].astype(BF16), row(rms2_g[l]),
               ffn_gate[l].astype(BF16), ffn_up[l].astype(BF16), ffn_down[l].astype(BF16), row(final_g))
    return out.reshape(batch, seq, D_MODEL)
```

```python
import functools
import math

import jax
import jax.numpy as jnp
from jax import lax
from jax.experimental import pallas as pl
from jax.experimental.pallas import tpu as pltpu

F32 = jnp.float32
BF16 = jnp.bfloat16

D_MODEL = 1024
W_RWKV = 512
HD_RWKV = 64
LORA_DECAY = 32
LORA_ICLR = 32
LORA_GATE = 96
GN_EPS = 64e-5
W_GLA = 512
DV_GLA = 128
H_GLA = 4
DK_GLA = 64
LORA_GK = 16
GK_NORMALIZER = 16.0
GLA_EPS = 1e-5
D_FF = 2816
RMS_EPS = 1e-6
CHUNK = 64

LANES = 128
PAIR_ROWS = 2 * CHUNK
LORA_PAD = 256
ZR_COLS = 3 * W_RWKV + LORA_PAD
GK_PAD = 128
ZG_COLS = 2 * H_GLA * DK_GLA + 2 * W_GLA + GK_PAD
QK_GLA = H_GLA * DK_GLA

TM_PROJ = 512
TB_MIX = 256
VMEM_LIMIT = 56 * 1024 * 1024


def _dot(a, b):
    return jnp.dot(a, b, preferred_element_type=F32)


def _dot_nt(a, b):
    return lax.dot_general(a, b, (((1,), (1,)), ((), ())), preferred_element_type=F32)


def _dot_tn(a, b):
    return lax.dot_general(a, b, (((0,), (0,)), ((), ())), preferred_element_type=F32)


def _split3(x):
    p0 = x.astype(BF16)
    r = x - p0.astype(F32)
    p1 = r.astype(BF16)
    p2 = (r - p1.astype(F32)).astype(BF16)
    return p0, p1, p2


def _dot_exact_rhs(m01, x):
    p0, p1, p2 = _split3(x)
    return _dot(m01, p0) + _dot(m01, p1) + _dot(m01, p2)


def _head_sums(x, ones_blocks):
    hi = x.astype(BF16)
    lo = (x - hi.astype(F32)).astype(BF16)
    width = ones_blocks.shape[0]
    parts = [_dot(hi[:, c:c + width], ones_blocks) + _dot(lo[:, c:c + width], ones_blocks)
             for c in range(0, x.shape[1], width)]
    return jnp.concatenate(parts, axis=1)


def _sigmoid(x):
    return 1.0 / (1.0 + jnp.exp(-x))


def _softplus(x):
    return jnp.maximum(x, 0.0) + jnp.log(1.0 + jnp.exp(-jnp.abs(x)))


def _stack_heads(x, lane_is_h0):
    zero = jnp.zeros_like(x)
    return jnp.concatenate([jnp.where(lane_is_h0, x, zero), jnp.where(lane_is_h0, zero, x)], axis=0)


def _round_robin(*generators):
    pending = list(generators)
    while pending:
        for gen in list(pending):
            try:
                next(gen)
            except StopIteration:
                pending.remove(gen)


def _prep_kernel(x_ref, g1_ref, wr_ref, wg_ref, mu_ref, db_ref, du_ref, ib_ref, iu_ref, gu_ref, kk_ref, ka_ref,
                 rk_ref, ones_ref, ltri_ref,
                 at_ref, rt_ref, bc_ref, kc_ref, bh_ref, kh_ref, v_ref, gate_ref, bonus_ref, wc_ref, zg_ref,
                 zprev_ref, *, tiles_per_seq):
    tm = x_ref.shape[0]
    half = ltri_ref.shape[0]

    @pl.when(pl.program_id(0) % tiles_per_seq == 0)
    def _():
        zprev_ref[...] = jnp.zeros_like(zprev_ref)

    x = x_ref[...]
    n = x * lax.rsqrt(jnp.mean(x * x, axis=-1, keepdims=True) + RMS_EPS) * g1_ref[...]
    n = n.astype(BF16)
    z_all = _dot(n, wr_ref[...])
    ones_blocks = ones_ref[...]
    row = lax.broadcasted_iota(jnp.int32, (half, 1), 0)

    def gla_columns():
        for c in range(0, ZG_COLS, 2 * LANES):
            w = min(2 * LANES, ZG_COLS - c)
            zg_ref[:, c:c + w] = _dot(n, wg_ref[:, c:c + w])
            yield

    def rwkv_rows(r0, carry):
        rows = slice(r0, r0 + half)
        z = z_all[rows]
        z_prev = jnp.where(row == 0, carry, pltpu.roll(z, 1, 0))
        zs = z + (z_prev - z) * mu_ref[...]
        r = zs[:, 0:W_RWKV]
        k = zs[:, W_RWKV:2 * W_RWKV]
        v = zs[:, 2 * W_RWKV:3 * W_RWKV]
        lo = zs[:, 3 * W_RWKV:ZR_COLS]
        v_ref[rows, :] = v.astype(BF16)
        yield
        xw = db_ref[...] + _dot(jnp.tanh(lo).astype(BF16), du_ref[...])
        ld = _sigmoid(xw) * (-math.exp(-0.5))
        a = _sigmoid(ib_ref[...] + _dot(lo.astype(BF16), iu_ref[...]))
        gate_ref[rows, :] = _dot(_sigmoid(lo).astype(BF16), gu_ref[...])
        yield
        kk = k * kk_ref[...]
        kk = kk * lax.rsqrt(jnp.maximum(_head_sums(kk * kk, ones_blocks), 1e-24))
        k2 = k * (1.0 + (a - 1.0) * ka_ref[...])
        bonus_ref[rows, :] = _head_sums(r * k2 * rk_ref[...], ones_blocks) * v
        bv = kk * a
        yield
        cw = _dot_exact_rhs(ltri_ref[...], ld)
        e_cw = jnp.exp(cw)
        e_neg = jnp.exp(-cw)
        at_ref[rows, :] = (-kk * jnp.exp(cw - ld)).astype(BF16)
        rt_ref[rows, :] = (r * e_cw).astype(BF16)
        yield
        bnc = bv * e_neg
        knc = k2 * e_neg
        bc_ref[rows, :] = bnc.astype(BF16)
        kc_ref[rows, :] = knc.astype(BF16)
        for c0 in range(0, half, CHUNK):
            w_c = e_cw[c0 + CHUNK - 1:c0 + CHUNK, :]
            wc_ref[(r0 + c0) // CHUNK:(r0 + c0) // CHUNK + 1, :] = w_c
            crows = slice(r0 + c0, r0 + c0 + CHUNK)
            bh_ref[crows, :] = (bnc[c0:c0 + CHUNK] * w_c).astype(BF16)
            kh_ref[crows, :] = (knc[c0:c0 + CHUNK] * w_c).astype(BF16)
        yield

    carries = [zprev_ref[...]] + [z_all[r0 - 1:r0, :] for r0 in range(half, tm, half)]
    zprev_ref[...] = z_all[tm - 1:tm, :]
    _round_robin(gla_columns(), *[rwkv_rows(i * half, c) for i, c in enumerate(carries)])


def _const_spec(shape):
    return pl.BlockSpec(shape, lambda *_: (0,) * len(shape), pipeline_mode=pl.Buffered(1))


def _prep(x2d, g1, w_r, w_g, mu, db, du, ib, iu, gu, k_k, k_a, r_k, ones_blocks, ltri, seq):
    n_tok = x2d.shape[0]
    tm = TM_PROJ
    row512 = _const_spec((1, W_RWKV))
    tile = lambda width: pl.BlockSpec((tm, width), lambda i: (i, 0))
    bf_out = jax.ShapeDtypeStruct((n_tok, W_RWKV), BF16)
    f32_out = jax.ShapeDtypeStruct((n_tok, W_RWKV), F32)
    return pl.pallas_call(
        functools.partial(_prep_kernel, tiles_per_seq=seq // tm),
        grid=(n_tok // tm,),
        in_specs=[
            tile(D_MODEL),
            _const_spec((1, D_MODEL)),
            _const_spec((D_MODEL, ZR_COLS)),
            _const_spec((D_MODEL, ZG_COLS)),
            _const_spec((1, ZR_COLS)),
            row512, _const_spec((LORA_PAD, W_RWKV)),
            row512, _const_spec((LORA_PAD, W_RWKV)),
            _const_spec((LORA_PAD, W_RWKV)),
            row512, row512, row512,
            _const_spec((2 * LANES, 2 * LANES)),
            _const_spec((TB_MIX, TB_MIX)),
        ],
        out_specs=[tile(W_RWKV)] * 9 + [
            pl.BlockSpec((tm // CHUNK, W_RWKV), lambda i: (i, 0)),
            tile(ZG_COLS),
        ],
        out_shape=[bf_out] * 7 + [f32_out] * 2 + [
            jax.ShapeDtypeStruct((n_tok // CHUNK, W_RWKV), F32),
            jax.ShapeDtypeStruct((n_tok, ZG_COLS), F32),
        ],
        scratch_shapes=[pltpu.VMEM((1, ZR_COLS), F32)],
        compiler_params=pltpu.CompilerParams(
            dimension_semantics=("arbitrary",), vmem_limit_bytes=VMEM_LIMIT),
        name="inproj_prep",
    )(x2d, g1, w_r, w_g, mu, db, du, ib, iu, gu, k_k, k_a, r_k, ones_blocks, ltri)


def _rwkv_kernel(at_ref, rt_ref, bc_ref, kc_ref, bh_ref, kh_ref, v_ref, gate_ref, bonus_ref, wc_ref,
                 lw_ref, lb_ref, ones_ref,
                 y_ref,
                 s_ref, ybuf_ref):
    tb = at_ref.shape[0]
    n_chunks = tb // CHUNK
    n_pairs = W_RWKV // LANES
    j = pl.program_id(1)

    @pl.when(j == 0)
    def _():
        s_ref[...] = jnp.zeros_like(s_ref)

    lane_is_h0 = lax.broadcasted_iota(jnp.int32, (CHUNK, LANES), 1) < HD_RWKV
    ti = lax.broadcasted_iota(jnp.int32, (CHUNK, LANES), 0)
    tj = lax.broadcasted_iota(jnp.int32, (CHUNK, LANES), 1) % CHUNK
    strict_lower = ti > tj
    incl_lower = ti >= tj
    eye = (ti == tj).astype(F32)
    ri = lax.broadcasted_iota(jnp.int32, (PAIR_ROWS, LANES), 0)
    ci = lax.broadcasted_iota(jnp.int32, (PAIR_ROWS, LANES), 1)
    same_head = (ri < HD_RWKV) == (ci < HD_RWKV)
    stack = lambda x: _stack_heads(x, lane_is_h0)
    bf = lambda x: x.astype(BF16)

    units = [(n, p) for n in range(n_chunks) for p in range(n_pairs)]
    rows_of = lambda n: pl.ds(n * CHUNK, CHUNK)
    lanes_of = lambda p: pl.ds(p * LANES, LANES)
    load = lambda ref: [ref[rows_of(n), lanes_of(p)] for n, p in units]
    at_l, rt_l, bc_l, kc_l, v_l, bh_l, kh_l = (
        load(ref) for ref in (at_ref, rt_ref, bc_ref, kc_ref, v_ref, bh_ref, kh_ref))

    x4 = [_dot_nt(jnp.concatenate([a_, r_], axis=0), jnp.concatenate([stack(b_), stack(k_)], axis=0))
          for a_, r_, b_, k_ in zip(at_l, rt_l, bc_l, kc_l)]
    a_ab = [jnp.where(strict_lower, x[0:CHUNK, 0:LANES], 0.0) for x in x4]
    a_k = [bf(jnp.concatenate([jnp.where(strict_lower, x[0:CHUNK, LANES:], 0.0),
                               jnp.where(incl_lower, x[CHUNK:, LANES:], 0.0)], axis=0)) for x in x4]
    a_rb = [bf(jnp.where(incl_lower, x[CHUNK:, 0:LANES], 0.0)) for x in x4]
    av_both = [_dot(a_, stack(v_)) for a_, v_ in zip(a_k, v_l)]

    t_inv = [eye + a_ for a_ in a_ab]
    a_pow = [bf(a_) for a_ in a_ab]
    a_pow = [bf(_dot(a_, stack(a_))) for a_ in a_pow]
    for _ in range(4):
        x2 = [_dot(a_, jnp.concatenate([stack(a_), stack(bf(t_))], axis=1)) for a_, t_ in zip(a_pow, t_inv)]
        t_inv = [t_ + x[:, LANES:] for t_, x in zip(t_inv, x2)]
        a_pow = [bf(x[:, 0:LANES]) for x in x2]
    t_inv = [t_ + _dot(a_, stack(bf(t_))) for t_, a_ in zip(t_inv, a_pow)]
    pq = [_dot(bf(t_), jnp.concatenate([stack(a_), stack(bf(avb[0:CHUNK]))], axis=1))
          for t_, a_, avb in zip(t_inv, at_l, av_both)]
    p_m = [bf(x[:, 0:LANES]) for x in pq]
    q_m = [x[:, LANES:] for x in pq]
    m_p = [bf(jnp.where(same_head, _dot_tn(p_, b_), 0.0)) for p_, b_ in zip(p_m, bh_l)]
    n_p = [jnp.where(same_head,
                     _dot_tn(jnp.concatenate([bf(q_), v_], axis=0), jnp.concatenate([b_, k_], axis=0)), 0.0)
           for q_, v_, b_, k_ in zip(q_m, v_l, bh_l, kh_l)]

    state = [s_ref[p] for p in range(n_pairs)]
    for n in range(n_chunks):
        ids = [n * n_pairs + p for p in range(n_pairs)]
        decay = wc_ref[pl.ds(j * n_chunks + n, 1), :]
        s_bf = [bf(s) for s in state]
        state = [s * decay[:, p * LANES:(p + 1) * LANES] + _dot(sb, m_p[i]) + n_p[i]
                 for p, (i, s, sb) in enumerate(zip(ids, state, s_bf))]
        uy = [_dot_nt(jnp.concatenate([p_m[i], rt_l[i]], axis=0), sb) for i, sb in zip(ids, s_bf)]
        u = [x[0:CHUNK] + q_m[i] for i, x in zip(ids, uy)]
        y_c = [x[CHUNK:] + _dot(a_rb[i], stack(bf(u_))) + av_both[i][CHUNK:] for i, x, u_ in zip(ids, uy, u)]
        for p in range(n_pairs):
            ybuf_ref[rows_of(n), lanes_of(p)] = y_c[p]
    for p in range(n_pairs):
        s_ref[p] = state[p]

    ones_blocks = ones_ref[...]
    y = ybuf_ref[...]
    inv_hd = 1.0 / HD_RWKV
    mean = _head_sums(y, ones_blocks) * inv_hd
    d = y - mean
    var = _head_sums(d * d, ones_blocks) * inv_hd
    yn = d * lax.rsqrt(var + GN_EPS) * lw_ref[...] + lb_ref[...]
    y_ref[...] = ((yn + bonus_ref[...]) * gate_ref[...]).astype(y_ref.dtype)


def _rwkv(prepared, wc, lw, lb, ones_blocks, batch, seq):
    tb = TB_MIX
    row512 = _const_spec((1, W_RWKV))
    slab = pl.BlockSpec((None, tb, W_RWKV), lambda b, j: (b, j, 0))
    prepared = [t.reshape(batch, seq, W_RWKV) for t in prepared]
    return pl.pallas_call(
        _rwkv_kernel,
        grid=(batch, seq // tb),
        in_specs=[slab] * 9 + [
            pl.BlockSpec((None, seq // CHUNK, W_RWKV), lambda b, j: (b, 0, 0)),
            row512, row512,
            _const_spec((2 * LANES, 2 * LANES)),
        ],
        out_specs=slab,
        out_shape=jax.ShapeDtypeStruct((batch, seq, W_RWKV), BF16),
        scratch_shapes=[
            pltpu.VMEM((W_RWKV // LANES, LANES, LANES), F32),
            pltpu.VMEM((tb, W_RWKV), F32),
        ],
        compiler_params=pltpu.CompilerParams(
            dimension_semantics=("parallel", "arbitrary"), vmem_limit_bytes=VMEM_LIMIT),
        name="rwkv7_mixer",
    )(*prepared, wc.reshape(batch, seq // CHUNK, W_RWKV), lw, lb, ones_blocks)


def _gla_kernel(z_ref, gu_ref, gb_ref, ng_ref, ltri_ref, lones_ref,
                y_ref,
                s_ref, qi_ref, ki_ref, ke_ref, v_ref, dec_ref, obuf_ref):
    tb = z_ref.shape[0]
    n_chunks = tb // CHUNK
    n_pairs = QK_GLA // LANES

    @pl.when(pl.program_id(1) == 0)
    def _():
        s_ref[...] = jnp.zeros_like(s_ref)

    q = z_ref[:, 0:QK_GLA] * (DK_GLA ** -0.5)
    k = z_ref[:, QK_GLA:2 * QK_GLA]
    lo = z_ref[:, 2 * QK_GLA + 2 * W_GLA:ZG_COLS]
    x = _dot(lo.astype(BF16), gu_ref[...]) + gb_ref[...]
    gk = -_softplus(-x) * (1.0 / GK_NORMALIZER)
    b = _dot_exact_rhs(ltri_ref[...], gk)
    b_last = _dot_exact_rhs(lones_ref[...], gk)
    qi_ref[...] = (q * jnp.exp(b)).astype(BF16)
    ki_ref[...] = (k * jnp.exp(-b)).astype(BF16)
    ke_ref[...] = (k * jnp.exp(b_last - b)).astype(BF16)
    v_ref[...] = z_ref[:, 2 * QK_GLA:2 * QK_GLA + W_GLA].astype(BF16)
    dec_ref[...] = jnp.exp(b_last)

    lane_is_h0 = lax.broadcasted_iota(jnp.int32, (CHUNK, LANES), 1) < DK_GLA
    ri = lax.broadcasted_iota(jnp.int32, (PAIR_ROWS, CHUNK), 0)
    ci = lax.broadcasted_iota(jnp.int32, (PAIR_ROWS, CHUNK), 1)
    causal = jnp.where(ri < CHUNK, ri, ri - CHUNK) >= ci

    for n in range(n_chunks):
        rows = pl.ds(n * CHUNK, CHUNK)
        for p in range(n_pairs):
            lanes = pl.ds(p * LANES, LANES)
            q_s = _stack_heads(qi_ref[rows, lanes], lane_is_h0)
            att = jnp.where(causal, _dot_nt(q_s, ki_ref[rows, lanes]), 0.0).astype(BF16)
            v0 = v_ref[rows, pl.ds((2 * p) * DV_GLA, DV_GLA)]
            v1 = v_ref[rows, pl.ds((2 * p + 1) * DV_GLA, DV_GLA)]
            s_old = s_ref[p]
            inter = _dot_nt(q_s, s_old.astype(BF16))
            obuf_ref[rows, pl.ds((2 * p) * DV_GLA, DV_GLA)] = _dot(att[0:CHUNK], v0) + inter[0:CHUNK]
            obuf_ref[rows, pl.ds((2 * p + 1) * DV_GLA, DV_GLA)] = _dot(att[CHUNK:], v1) + inter[CHUNK:]
            ke_s = _stack_heads(ke_ref[rows, lanes], lane_is_h0)
            ds = _dot_tn(jnp.concatenate([v0, v1], axis=0), ke_s)
            s_ref[p] = s_old * dec_ref[pl.ds(n * CHUNK, 1), lanes] + ds

    ng = ng_ref[...]
    for h in range(H_GLA):
        cols = pl.ds(h * DV_GLA, DV_GLA)
        o = obuf_ref[:, cols]
        og = z_ref[:, pl.ds(2 * QK_GLA + W_GLA + h * DV_GLA, DV_GLA)]
        o = o * lax.rsqrt(jnp.mean(o * o, axis=-1, keepdims=True) + GLA_EPS) * ng
        y_ref[:, cols] = (o * (og * _sigmoid(og))).astype(y_ref.dtype)


def _gla(zg, gk_up, gk_bias, norm_g, ltri, lones, batch, seq):
    tb = TB_MIX
    return pl.pallas_call(
        _gla_kernel,
        grid=(batch, seq // tb),
        in_specs=[
            pl.BlockSpec((None, tb, ZG_COLS), lambda b, j: (b, j, 0)),
            _const_spec((GK_PAD, QK_GLA)),
            _const_spec((1, QK_GLA)),
            _const_spec((1, DV_GLA)),
            _const_spec((tb, tb)),
            _const_spec((tb, tb)),
        ],
        out_specs=pl.BlockSpec((None, tb, W_GLA), lambda b, j: (b, j, 0)),
        out_shape=jax.ShapeDtypeStruct((batch, seq, W_GLA), BF16),
        scratch_shapes=[
            pltpu.VMEM((QK_GLA // LANES, DV_GLA, LANES), F32),
            pltpu.VMEM((tb, QK_GLA), BF16),
            pltpu.VMEM((tb, QK_GLA), BF16),
            pltpu.VMEM((tb, QK_GLA), BF16),
            pltpu.VMEM((tb, W_GLA), BF16),
            pltpu.VMEM((tb, QK_GLA), F32),
            pltpu.VMEM((tb, W_GLA), F32),
        ],
        compiler_params=pltpu.CompilerParams(
            dimension_semantics=("parallel", "arbitrary"), vmem_limit_bytes=VMEM_LIMIT),
        name="gla_mixer",
    )(zg, gk_up, gk_bias, norm_g, ltri, lones)


def _ffn_kernel(x_ref, yr_ref, yg_ref, wor_ref, wog_ref, g2_ref, wg_ref, wu_ref, wd_ref, gf_ref, o_ref):
    h = x_ref[...] + _dot(yr_ref[...], wor_ref[...]) + _dot(yg_ref[...], wog_ref[...])
    n = h * lax.rsqrt(jnp.mean(h * h, axis=-1, keepdims=True) + RMS_EPS) * g2_ref[...]
    n = n.astype(BF16)
    gate = _dot(n, wg_ref[...])
    up = _dot(n, wu_ref[...])
    act = (gate * _sigmoid(gate) * up).astype(BF16)
    h = h + _dot(act, wd_ref[...])
    o_ref[...] = h * lax.rsqrt(jnp.mean(h * h, axis=-1, keepdims=True) + RMS_EPS) * gf_ref[...]


def _ffn(x2d, yr, yg, wo_r, wo_g, g2, wg, wu, wd, gf):
    n_tok = x2d.shape[0]
    tm = TM_PROJ
    return pl.pallas_call(
        _ffn_kernel,
        grid=(n_tok // tm,),
        in_specs=[
            pl.BlockSpec((tm, D_MODEL), lambda i: (i, 0)),
            pl.BlockSpec((tm, W_RWKV), lambda i: (i, 0)),
            pl.BlockSpec((tm, W_GLA), lambda i: (i, 0)),
            _const_spec((W_RWKV, D_MODEL)),
            _const_spec((W_GLA, D_MODEL)),
            _const_spec((1, D_MODEL)),
            _const_spec((D_MODEL, D_FF)),
            _const_spec((D_MODEL, D_FF)),
            _const_spec((D_FF, D_MODEL)),
            _const_spec((1, D_MODEL)),
        ],
        out_specs=pl.BlockSpec((tm, D_MODEL), lambda i: (i, 0)),
        out_shape=jax.ShapeDtypeStruct((n_tok, D_MODEL), F32),
        compiler_params=pltpu.CompilerParams(
            dimension_semantics=("parallel",), vmem_limit_bytes=VMEM_LIMIT),
        name="outproj_ffn",
    )(x2d, yr, yg, wo_r, wo_g, g2, wg, wu, wd, gf)


def _pad_cols(w, width):
    return jnp.pad(w, ((0, 0), (0, width - w.shape[1])))


def _block_tri(tb):
    i = jnp.arange(tb)
    same = (i[:, None] // CHUNK) == (i[None, :] // CHUNK)
    ltri = (same & (i[:, None] >= i[None, :])).astype(BF16)
    return ltri, same.astype(BF16)


def kernel(x, rms1_g, w_in, mu_shift, decay_base, decay_up, iclr_base, iclr_up, gate_up, k_k, k_a, r_k,
           lnx_w, lnx_b, gk_up, gk_bias, gla_norm_g, w_out, rms2_g, ffn_gate, ffn_up, ffn_down, final_g):
    batch, seq, _ = x.shape
    depth = w_in.shape[0]
    d_rwkv_in = 3 * W_RWKV + LORA_DECAY + LORA_ICLR + LORA_GATE
    d_gla_in = 2 * QK_GLA + 2 * W_GLA + LORA_GK
    row = lambda t: t.reshape(1, -1).astype(F32)

    ltri, lones = _block_tri(TB_MIX)
    hi = jnp.arange(2 * LANES) // HD_RWKV
    ones_blocks = (hi[:, None] == hi[None, :]).astype(BF16)

    assert depth == 1, "single-layer trunk: the final norm is fused into the layer's last kernel"
    l = 0
    h = x.reshape(batch * seq, D_MODEL)

    w_r = _pad_cols(w_in[l][:, :d_rwkv_in], ZR_COLS).astype(BF16)
    w_g = _pad_cols(w_in[l][:, d_rwkv_in:d_rwkv_in + d_gla_in], ZG_COLS).astype(BF16)

    def lora_rows(w, start):
        return jnp.zeros((LORA_PAD, W_RWKV), F32).at[start:start + w.shape[0]].set(w).astype(BF16)

    du = lora_rows(decay_up[l], 0)
    iu = lora_rows(iclr_up[l], LORA_DECAY)
    gu = lora_rows(gate_up[l], LORA_DECAY + LORA_ICLR)
    mu = _pad_cols(row(mu_shift[l]), ZR_COLS)
    *prepared, wc, zg = _prep(h, row(rms1_g[l]), w_r, w_g, mu, row(decay_base[l]), du, row(iclr_base[l]), iu, gu,
                              row(k_k[l]), row(k_a[l]), row(r_k[l]), ones_blocks, ltri, seq)
    y_rwkv = _rwkv(prepared, wc, row(lnx_w[l]), row(lnx_b[l]), ones_blocks, batch, seq)

    gku = jnp.zeros((GK_PAD, QK_GLA), F32).at[0:LORA_GK].set(gk_up[l]).astype(BF16)
    y_gla = _gla(zg.reshape(batch, seq, ZG_COLS), gku, row(gk_bias[l]), row(gla_norm_g[l]),
                 ltri, lones, batch, seq)

    out = _ffn(h, y_rwkv.reshape(batch * seq, W_RWKV), y_gla.reshape(batch * seq, W_GLA),
               w_out[l][:W_RWKV].astype(BF16), w_out[l][W_RWKV:].astype(BF16), row(rms2_g[l]),
               ffn_gate[l].astype(BF16), ffn_up[l].astype(BF16), ffn_down[l].astype(BF16), row(final_g))
    return out.reshape(batch, seq, D_MODEL)
```

```python
import functools
import math

import jax
import jax.numpy as jnp
from jax import lax
from jax.experimental import pallas as pl
from jax.experimental.pallas import tpu as pltpu

F32 = jnp.float32
BF16 = jnp.bfloat16

D_MODEL = 1024
W_RWKV = 512
HD_RWKV = 64
LORA_DECAY = 32
LORA_ICLR = 32
LORA_GATE = 96
GN_EPS = 64e-5
W_GLA = 512
DV_GLA = 128
H_GLA = 4
DK_GLA = 64
LORA_GK = 16
GK_NORMALIZER = 16.0
GLA_EPS = 1e-5
D_FF = 2816
RMS_EPS = 1e-6
CHUNK = 64

LANES = 128
PAIR_ROWS = 2 * CHUNK
LORA_PAD = 256
ZR_COLS = 3 * W_RWKV + LORA_PAD
GK_PAD = 128
ZG_COLS = 2 * H_GLA * DK_GLA + 2 * W_GLA + GK_PAD
QK_GLA = H_GLA * DK_GLA

TM_PROJ = 512
TB_MIX = 256
VMEM_LIMIT = 56 * 1024 * 1024


def _dot(a, b):
    return jnp.dot(a, b, preferred_element_type=F32)


def _dot_nt(a, b):
    return lax.dot_general(a, b, (((1,), (1,)), ((), ())), preferred_element_type=F32)


def _dot_tn(a, b):
    return lax.dot_general(a, b, (((0,), (0,)), ((), ())), preferred_element_type=F32)


def _split3(x):
    p0 = x.astype(BF16)
    r = x - p0.astype(F32)
    p1 = r.astype(BF16)
    p2 = (r - p1.astype(F32)).astype(BF16)
    return p0, p1, p2


def _dot_exact_rhs(m01, x):
    p0, p1, p2 = _split3(x)
    return _dot(m01, p0) + _dot(m01, p1) + _dot(m01, p2)


def _head_sums(x, ones_blocks):
    hi = x.astype(BF16)
    lo = (x - hi.astype(F32)).astype(BF16)
    width = ones_blocks.shape[0]
    parts = [_dot(hi[:, c:c + width], ones_blocks) + _dot(lo[:, c:c + width], ones_blocks)
             for c in range(0, x.shape[1], width)]
    return jnp.concatenate(parts, axis=1)


def _sigmoid(x):
    return 1.0 / (1.0 + jnp.exp(-x))


def _softplus(x):
    return jnp.maximum(x, 0.0) + jnp.log(1.0 + jnp.exp(-jnp.abs(x)))


def _stack_heads(x, lane_is_h0):
    zero = jnp.zeros_like(x)
    return jnp.concatenate([jnp.where(lane_is_h0, x, zero), jnp.where(lane_is_h0, zero, x)], axis=0)


def _prep_kernel(x_ref, g1_ref, wr_ref, wg_ref, mu_ref, db_ref, du_ref, ib_ref, iu_ref, gu_ref, kk_ref, ka_ref,
                 rk_ref, ones_ref, ltri_ref,
                 at_ref, rt_ref, bc_ref, kc_ref, bh_ref, kh_ref, v_ref, gate_ref, bonus_ref, wc_ref, zg_ref,
                 zprev_ref, *, tiles_per_seq):
    tm = x_ref.shape[0]
    half = ltri_ref.shape[0]

    @pl.when(pl.program_id(0) % tiles_per_seq == 0)
    def _():
        zprev_ref[...] = jnp.zeros_like(zprev_ref)

    x = x_ref[...]
    n = x * lax.rsqrt(jnp.mean(x * x, axis=-1, keepdims=True) + RMS_EPS) * g1_ref[...]
    n = n.astype(BF16)
    ones_blocks = ones_ref[...]
    row = lax.broadcasted_iota(jnp.int32, (half, 1), 0)
    cols_of = {"r": (0, W_RWKV), "k": (W_RWKV, 2 * W_RWKV), "v": (2 * W_RWKV, 3 * W_RWKV),
               "lo": (3 * W_RWKV, ZR_COLS)}
    z = {}

    def project(name):
        c0, c1 = cols_of[name]
        z[name] = _dot(n, wr_ref[:, c0:c1])

    def gla_columns(c):
        w = min(2 * LANES, ZG_COLS - c)
        zg_ref[:, c:c + w] = _dot(n, wg_ref[:, c:c + w])

    def shifted(name, r0):
        c0, c1 = cols_of[name]
        cur = z[name][r0:r0 + half]
        carry = zprev_ref[:, c0:c1] if r0 == 0 else z[name][r0 - 1:r0]
        prev = jnp.where(row == 0, carry, pltpu.roll(cur, 1, 0))
        return cur + (prev - cur) * mu_ref[:, c0:c1]

    keep = [{} for _ in range(tm // half)]

    def decay_part(h):
        r0, rows, st = h * half, slice(h * half, (h + 1) * half), keep[h]
        lo = shifted("lo", r0)
        xw = db_ref[...] + _dot(jnp.tanh(lo).astype(BF16), du_ref[...])
        ld = _sigmoid(xw) * (-math.exp(-0.5))
        st["a"] = _sigmoid(ib_ref[...] + _dot(lo.astype(BF16), iu_ref[...]))
        gate_ref[rows, :] = _dot(_sigmoid(lo).astype(BF16), gu_ref[...])
        cw = _dot_exact_rhs(ltri_ref[...], ld)
        st["e_cw"] = jnp.exp(cw)
        st["e_neg"] = jnp.exp(-cw)
        st["e_prev"] = jnp.exp(cw - ld)
        for c0 in range(0, half, CHUNK):
            wc_ref[(r0 + c0) // CHUNK:(r0 + c0) // CHUNK + 1, :] = st["e_cw"][c0 + CHUNK - 1:c0 + CHUNK, :]

    def key_part(h):
        r0, rows, st = h * half, slice(h * half, (h + 1) * half), keep[h]
        k = shifted("k", r0)
        kk = k * kk_ref[...]
        kk = kk * lax.rsqrt(jnp.maximum(_head_sums(kk * kk, ones_blocks), 1e-24))
        st["k2"] = k2 = k * (1.0 + (st["a"] - 1.0) * ka_ref[...])
        at_ref[rows, :] = (-kk * st["e_prev"]).astype(BF16)
        bnc = kk * st["a"] * st["e_neg"]
        knc = k2 * st["e_neg"]
        bc_ref[rows, :] = bnc.astype(BF16)
        kc_ref[rows, :] = knc.astype(BF16)
        for c0 in range(0, half, CHUNK):
            w_c = st["e_cw"][c0 + CHUNK - 1:c0 + CHUNK, :]
            crows = slice(r0 + c0, r0 + c0 + CHUNK)
            bh_ref[crows, :] = (bnc[c0:c0 + CHUNK] * w_c).astype(BF16)
            kh_ref[crows, :] = (knc[c0:c0 + CHUNK] * w_c).astype(BF16)

    def receptance_part(h):
        rows, st = slice(h * half, (h + 1) * half), keep[h]
        r = shifted("r", h * half)
        rt_ref[rows, :] = (r * st["e_cw"]).astype(BF16)
        st["rk"] = _head_sums(r * st["k2"] * rk_ref[...], ones_blocks)

    def value_part(h):
        rows, st = slice(h * half, (h + 1) * half), keep[h]
        v = shifted("v", h * half)
        v_ref[rows, :] = v.astype(BF16)
        bonus_ref[rows, :] = st["rk"] * v

    halves = range(tm // half)
    token_work = ([functools.partial(decay_part, h) for h in halves] + [functools.partial(key_part, h) for h in halves]
                  + [functools.partial(receptance_part, h) for h in halves]
                  + [functools.partial(value_part, h) for h in halves])
    projections = ([functools.partial(project, name) for name in ("lo", "k", "r", "v")]
                   + [functools.partial(gla_columns, c) for c in range(0, ZG_COLS, 2 * LANES)])
    projections.pop(0)()
    projections.pop(0)()
    while token_work or projections:
        if token_work:
            token_work.pop(0)()
        if projections:
            projections.pop(0)()
    for name, (c0, c1) in cols_of.items():
        zprev_ref[:, c0:c1] = z[name][tm - 1:tm, :]


def _const_spec(shape):
    return pl.BlockSpec(shape, lambda *_: (0,) * len(shape), pipeline_mode=pl.Buffered(1))


def _prep(x2d, g1, w_r, w_g, mu, db, du, ib, iu, gu, k_k, k_a, r_k, ones_blocks, ltri, seq):
    n_tok = x2d.shape[0]
    tm = TM_PROJ
    row512 = _const_spec((1, W_RWKV))
    tile = lambda width: pl.BlockSpec((tm, width), lambda i: (i, 0))
    bf_out = jax.ShapeDtypeStruct((n_tok, W_RWKV), BF16)
    f32_out = jax.ShapeDtypeStruct((n_tok, W_RWKV), F32)
    return pl.pallas_call(
        functools.partial(_prep_kernel, tiles_per_seq=seq // tm),
        grid=(n_tok // tm,),
        in_specs=[
            tile(D_MODEL),
            _const_spec((1, D_MODEL)),
            _const_spec((D_MODEL, ZR_COLS)),
            _const_spec((D_MODEL, ZG_COLS)),
            _const_spec((1, ZR_COLS)),
            row512, _const_spec((LORA_PAD, W_RWKV)),
            row512, _const_spec((LORA_PAD, W_RWKV)),
            _const_spec((LORA_PAD, W_RWKV)),
            row512, row512, row512,
            _const_spec((2 * LANES, 2 * LANES)),
            _const_spec((TB_MIX, TB_MIX)),
        ],
        out_specs=[tile(W_RWKV)] * 9 + [
            pl.BlockSpec((tm // CHUNK, W_RWKV), lambda i: (i, 0)),
            tile(ZG_COLS),
        ],
        out_shape=[bf_out] * 7 + [f32_out] * 2 + [
            jax.ShapeDtypeStruct((n_tok // CHUNK, W_RWKV), F32),
            jax.ShapeDtypeStruct((n_tok, ZG_COLS), F32),
        ],
        scratch_shapes=[pltpu.VMEM((1, ZR_COLS), F32)],
        compiler_params=pltpu.CompilerParams(
            dimension_semantics=("arbitrary",), vmem_limit_bytes=VMEM_LIMIT),
        name="inproj_prep",
    )(x2d, g1, w_r, w_g, mu, db, du, ib, iu, gu, k_k, k_a, r_k, ones_blocks, ltri)


def _rwkv_steps(at_ref, rt_ref, bc_ref, kc_ref, bh_ref, kh_ref, v_ref, gate_ref, bonus_ref, wc_ref,
                lw_ref, lb_ref, ones_ref, y_ref, s_ref, ybuf_ref):
    tb = at_ref.shape[0]
    n_chunks = tb // CHUNK
    n_pairs = W_RWKV // LANES
    j = pl.program_id(1)

    lane_is_h0 = lax.broadcasted_iota(jnp.int32, (CHUNK, LANES), 1) < HD_RWKV
    ti = lax.broadcasted_iota(jnp.int32, (CHUNK, LANES), 0)
    tj = lax.broadcasted_iota(jnp.int32, (CHUNK, LANES), 1) % CHUNK
    strict_lower = ti > tj
    incl_lower = ti >= tj
    eye = (ti == tj).astype(F32)
    ri = lax.broadcasted_iota(jnp.int32, (PAIR_ROWS, LANES), 0)
    ci = lax.broadcasted_iota(jnp.int32, (PAIR_ROWS, LANES), 1)
    same_head = (ri < HD_RWKV) == (ci < HD_RWKV)
    stack = lambda x: _stack_heads(x, lane_is_h0)
    bf = lambda x: x.astype(BF16)

    units = [(n, p) for n in range(n_chunks) for p in range(n_pairs)]
    rows_of = lambda n: pl.ds(n * CHUNK, CHUNK)
    lanes_of = lambda p: pl.ds(p * LANES, LANES)
    load = lambda ref: [ref[rows_of(n), lanes_of(p)] for n, p in units]
    w = {}

    def cross_products():
        for name, ref in (("at", at_ref), ("rt", rt_ref), ("bc", bc_ref), ("kc", kc_ref), ("v", v_ref),
                          ("bh", bh_ref), ("kh", kh_ref)):
            w[name] = load(ref)
        w["x4"] = [_dot_nt(jnp.concatenate([a_, r_], axis=0), jnp.concatenate([stack(b_), stack(k_)], axis=0))
                   for a_, r_, b_, k_ in zip(w["at"], w["rt"], w["bc"], w["kc"])]

    def masks():
        x4 = w.pop("x4")
        w["a_ab"] = [jnp.where(strict_lower, x[0:CHUNK, 0:LANES], 0.0) for x in x4]
        w["a_k"] = [bf(jnp.concatenate([jnp.where(strict_lower, x[0:CHUNK, LANES:], 0.0),
                                        jnp.where(incl_lower, x[CHUNK:, LANES:], 0.0)], axis=0)) for x in x4]
        w["a_rb"] = [bf(jnp.where(incl_lower, x[CHUNK:, 0:LANES], 0.0)) for x in x4]

    def values_products():
        w["av_both"] = [_dot(a_, stack(v_)) for a_, v_ in zip(w.pop("a_k"), w["v"])]

    def power_2():
        a_ab = w.pop("a_ab")
        w["t_inv"] = [eye + a_ for a_ in a_ab]
        w["a_pow"] = [bf(_dot(a_, stack(a_))) for a_ in map(bf, a_ab)]

    def next_power():
        x2 = [_dot(a_, jnp.concatenate([stack(a_), stack(bf(t_))], axis=1)) for a_, t_ in zip(w["a_pow"], w["t_inv"])]
        w["t_inv"] = [t_ + x[:, LANES:] for t_, x in zip(w["t_inv"], x2)]
        w["a_pow"] = [bf(x[:, 0:LANES]) for x in x2]

    def inverse_applied():
        t_inv = [t_ + _dot(a_, stack(bf(t_))) for t_, a_ in zip(w.pop("t_inv"), w.pop("a_pow"))]
        pq = [_dot(bf(t_), jnp.concatenate([stack(a_), stack(bf(avb[0:CHUNK]))], axis=1))
              for t_, a_, avb in zip(t_inv, w["at"], w["av_both"])]
        w["p_m"] = [bf(x[:, 0:LANES]) for x in pq]
        w["q_m"] = [x[:, LANES:] for x in pq]

    def transitions():
        w["m_p"] = [bf(jnp.where(same_head, _dot_tn(p_, b_), 0.0)) for p_, b_ in zip(w["p_m"], w["bh"])]
        w["n_p"] = [jnp.where(same_head, _dot_tn(jnp.concatenate([bf(q_), v_], axis=0),
                                                 jnp.concatenate([b_, k_], axis=0)), 0.0)
                    for q_, v_, b_, k_ in zip(w["q_m"], w["v"], w["bh"], w["kh"])]
        w["state"] = [s_ref[p] for p in range(n_pairs)]

    def chunk_step(n):
        ids = [n * n_pairs + p for p in range(n_pairs)]
        decay = wc_ref[pl.ds(j * n_chunks + n, 1), :]
        s_bf = [bf(s) for s in w["state"]]
        w["state"] = [s * decay[:, p * LANES:(p + 1) * LANES] + _dot(sb, w["m_p"][i]) + w["n_p"][i]
                      for p, (i, s, sb) in enumerate(zip(ids, w["state"], s_bf))]
        uy = [_dot_nt(jnp.concatenate([w["p_m"][i], w["rt"][i]], axis=0), sb) for i, sb in zip(ids, s_bf)]
        u = [x[0:CHUNK] + w["q_m"][i] for i, x in zip(ids, uy)]
        y_c = [x[CHUNK:] + _dot(w["a_rb"][i], stack(bf(u_))) + w["av_both"][i][CHUNK:]
               for i, x, u_ in zip(ids, uy, u)]
        for p in range(n_pairs):
            ybuf_ref[rows_of(n), lanes_of(p)] = y_c[p]

    def finish():
        for p in range(n_pairs):
            s_ref[p] = w["state"][p]
        ones_blocks = ones_ref[...]
        y = ybuf_ref[...]
        inv_hd = 1.0 / HD_RWKV
        mean = _head_sums(y, ones_blocks) * inv_hd
        d = y - mean
        var = _head_sums(d * d, ones_blocks) * inv_hd
        yn = d * lax.rsqrt(var + GN_EPS) * lw_ref[...] + lb_ref[...]
        y_ref[:, 0:W_RWKV] = ((yn + bonus_ref[...]) * gate_ref[...]).astype(y_ref.dtype)

    first = [cross_products, masks, values_products, power_2] + [next_power] * 4 + [inverse_applied, transitions]
    return first, [functools.partial(chunk_step, n) for n in range(n_chunks)] + [finish]


def _gla_steps(z_ref, gu_ref, gb_ref, ng_ref, ltri_ref, y_ref, s_ref):
    tb = z_ref.shape[0]
    n_chunks = tb // CHUNK
    n_pairs = QK_GLA // LANES
    units = [(n, p) for n in range(n_chunks) for p in range(n_pairs)]
    lane_is_h0 = lax.broadcasted_iota(jnp.int32, (CHUNK, LANES), 1) < DK_GLA
    ri = lax.broadcasted_iota(jnp.int32, (PAIR_ROWS, CHUNK), 0)
    ci = lax.broadcasted_iota(jnp.int32, (PAIR_ROWS, CHUNK), 1)
    causal = jnp.where(ri < CHUNK, ri, ri - CHUNK) >= ci
    stack = lambda x: _stack_heads(x, lane_is_h0)
    w = {}

    def gate_logits():
        lo = z_ref[:, 2 * QK_GLA + 2 * W_GLA:ZG_COLS]
        x = _dot(lo.astype(BF16), gu_ref[...]) + gb_ref[...]
        w["gk"] = -_softplus(-x) * (1.0 / GK_NORMALIZER)

    def cumulative_gates():
        b = _dot_exact_rhs(ltri_ref[...], w.pop("gk"))
        w["e_b"] = jnp.exp(b)
        w["e_nb"] = jnp.exp(-b)

    def scaled_operands():
        e_b, e_nb = w.pop("e_b"), w.pop("e_nb")
        q_in = (z_ref[:, 0:QK_GLA] * (DK_GLA ** -0.5) * e_b).astype(BF16)
        k_in = z_ref[:, QK_GLA:2 * QK_GLA] * e_nb
        k_in_bf = k_in.astype(BF16)
        v = z_ref[:, 2 * QK_GLA:2 * QK_GLA + W_GLA].astype(BF16)
        chunk = lambda t, n, c0, width: t[n * CHUNK:(n + 1) * CHUNK, c0:c0 + width]
        w["decay"] = [e_b[(n + 1) * CHUNK - 1:(n + 1) * CHUNK, :] for n in range(n_chunks)]
        w["q"] = [stack(chunk(q_in, n, p * LANES, LANES)) for n, p in units]
        w["k"] = [chunk(k_in_bf, n, p * LANES, LANES) for n, p in units]
        w["k_end"] = [stack((chunk(k_in, n, p * LANES, LANES)
                             * w["decay"][n][:, p * LANES:(p + 1) * LANES]).astype(BF16)) for n, p in units]
        w["v0"] = [chunk(v, n, (2 * p) * DV_GLA, DV_GLA) for n, p in units]
        w["v1"] = [chunk(v, n, (2 * p + 1) * DV_GLA, DV_GLA) for n, p in units]

    def scores():
        w["att"] = [jnp.where(causal, _dot_nt(q_, k_), 0.0).astype(BF16) for q_, k_ in zip(w["q"], w["k"])]

    def state_updates():
        ds = [_dot_tn(jnp.concatenate([v0, v1], axis=0), ke) for v0, v1, ke in zip(w["v0"], w["v1"], w["k_end"])]
        states = [s_ref[p] for p in range(n_pairs)]
        w["s_in"] = []
        for n in range(n_chunks):
            w["s_in"] += states
            states = [s * w["decay"][n][:, p * LANES:(p + 1) * LANES] + ds[n * n_pairs + p]
                      for p, s in enumerate(states)]
        for p in range(n_pairs):
            s_ref[p] = states[p]

    def outputs():
        intra0 = [_dot(a_[0:CHUNK], v_) for a_, v_ in zip(w["att"], w["v0"])]
        intra1 = [_dot(a_[CHUNK:], v_) for a_, v_ in zip(w["att"], w["v1"])]
        inter = [_dot_nt(q_, s_.astype(BF16)) for q_, s_ in zip(w["q"], w["s_in"])]
        w["o"] = {}
        for i, (n, p) in enumerate(units):
            w["o"][(n, 2 * p)] = intra0[i] + inter[i][0:CHUNK]
            w["o"][(n, 2 * p + 1)] = intra1[i] + inter[i][CHUNK:]

    def finish():
        ng = ng_ref[...]
        for h in range(H_GLA):
            o = jnp.concatenate([w["o"][(n, h)] for n in range(n_chunks)], axis=0)
            og = z_ref[:, pl.ds(2 * QK_GLA + W_GLA + h * DV_GLA, DV_GLA)]
            o = o * lax.rsqrt(jnp.mean(o * o, axis=-1, keepdims=True) + GLA_EPS) * ng
            y_ref[:, pl.ds(W_RWKV + h * DV_GLA, DV_GLA)] = (o * (og * _sigmoid(og))).astype(y_ref.dtype)

    return [gate_logits, cumulative_gates, scaled_operands], [scores, state_updates, outputs, finish]


def _mixer_kernel(at_ref, rt_ref, bc_ref, kc_ref, bh_ref, kh_ref, v_ref, gate_ref, bonus_ref, wc_ref,
                  lw_ref, lb_ref, ones_ref, zg_ref, gu_ref, gb_ref, ng_ref, ltri_ref,
                  y_ref,
                  s_rwkv_ref, ybuf_ref, s_gla_ref):
    @pl.when(pl.program_id(1) == 0)
    def _():
        s_rwkv_ref[...] = jnp.zeros_like(s_rwkv_ref)
        s_gla_ref[...] = jnp.zeros_like(s_gla_ref)

    rwkv_first, rwkv_chain = _rwkv_steps(at_ref, rt_ref, bc_ref, kc_ref, bh_ref, kh_ref, v_ref, gate_ref, bonus_ref,
                                         wc_ref, lw_ref, lb_ref, ones_ref, y_ref, s_rwkv_ref, ybuf_ref)
    gla_tokens, gla_chunks = _gla_steps(zg_ref, gu_ref, gb_ref, ng_ref, ltri_ref, y_ref, s_gla_ref)
    for group_a, group_b in ((rwkv_first, gla_tokens), (rwkv_chain, gla_chunks)):
        group_a, group_b = list(group_a), list(group_b)
        while group_a or group_b:
            if group_a:
                group_a.pop(0)()
            if group_b:
                group_b.pop(0)()


def _mixers(prepared, wc, zg, lw, lb, ones_blocks, gk_up, gk_bias, norm_g, ltri, batch, seq):
    tb = TB_MIX
    row512 = _const_spec((1, W_RWKV))
    slab = pl.BlockSpec((None, tb, W_RWKV), lambda b, j: (b, j, 0))
    prepared = [t.reshape(batch, seq, W_RWKV) for t in prepared]
    return pl.pallas_call(
        _mixer_kernel,
        grid=(batch, seq // tb),
        in_specs=[slab] * 9 + [
            pl.BlockSpec((None, seq // CHUNK, W_RWKV), lambda b, j: (b, 0, 0)),
            row512, row512,
            _const_spec((2 * LANES, 2 * LANES)),
            pl.BlockSpec((None, tb, ZG_COLS), lambda b, j: (b, j, 0)),
            _const_spec((GK_PAD, QK_GLA)),
            _const_spec((1, QK_GLA)),
            _const_spec((1, DV_GLA)),
            _const_spec((tb, tb)),
        ],
        out_specs=pl.BlockSpec((None, tb, D_MODEL), lambda b, j: (b, j, 0)),
        out_shape=jax.ShapeDtypeStruct((batch, seq, D_MODEL), BF16),
        scratch_shapes=[
            pltpu.VMEM((W_RWKV // LANES, LANES, LANES), F32),
            pltpu.VMEM((tb, W_RWKV), F32),
            pltpu.VMEM((QK_GLA // LANES, DV_GLA, LANES), F32),
        ],
        compiler_params=pltpu.CompilerParams(
            dimension_semantics=("parallel", "arbitrary"), vmem_limit_bytes=VMEM_LIMIT),
        name="mixers",
    )(*prepared, wc.reshape(batch, seq // CHUNK, W_RWKV), lw, lb, ones_blocks,
      zg.reshape(batch, seq, ZG_COLS), gk_up, gk_bias, norm_g, ltri)


def _ffn_kernel(x_ref, mix_ref, wo_ref, g2_ref, wg_ref, wu_ref, wd_ref, gf_ref, o_ref):
    h = x_ref[...] + _dot(mix_ref[...], wo_ref[...])
    n = h * lax.rsqrt(jnp.mean(h * h, axis=-1, keepdims=True) + RMS_EPS) * g2_ref[...]
    n = n.astype(BF16)
    gate = _dot(n, wg_ref[...])
    up = _dot(n, wu_ref[...])
    act = (gate * _sigmoid(gate) * up).astype(BF16)
    h = h + _dot(act, wd_ref[...])
    o_ref[...] = h * lax.rsqrt(jnp.mean(h * h, axis=-1, keepdims=True) + RMS_EPS) * gf_ref[...]


def _ffn(x2d, mix, wo, g2, wg, wu, wd, gf):
    n_tok = x2d.shape[0]
    tm = TM_PROJ
    return pl.pallas_call(
        _ffn_kernel,
        grid=(n_tok // tm,),
        in_specs=[
            pl.BlockSpec((tm, D_MODEL), lambda i: (i, 0)),
            pl.BlockSpec((tm, D_MODEL), lambda i: (i, 0)),
            _const_spec((D_MODEL, D_MODEL)),
            _const_spec((1, D_MODEL)),
            _const_spec((D_MODEL, D_FF)),
            _const_spec((D_MODEL, D_FF)),
            _const_spec((D_FF, D_MODEL)),
            _const_spec((1, D_MODEL)),
        ],
        out_specs=pl.BlockSpec((tm, D_MODEL), lambda i: (i, 0)),
        out_shape=jax.ShapeDtypeStruct((n_tok, D_MODEL), F32),
        compiler_params=pltpu.CompilerParams(
            dimension_semantics=("parallel",), vmem_limit_bytes=VMEM_LIMIT),
        name="outproj_ffn",
    )(x2d, mix, wo, g2, wg, wu, wd, gf)


def _pad_cols(w, width):
    return jnp.pad(w, ((0, 0), (0, width - w.shape[1])))


def _block_tri(tb):
    i = jnp.arange(tb)
    same = (i[:, None] // CHUNK) == (i[None, :] // CHUNK)
    return (same & (i[:, None] >= i[None, :])).astype(BF16)


def kernel(x, rms1_g, w_in, mu_shift, decay_base, decay_up, iclr_base, iclr_up, gate_up, k_k, k_a, r_k,
           lnx_w, lnx_b, gk_up, gk_bias, gla_norm_g, w_out, rms2_g, ffn_gate, ffn_up, ffn_down, final_g):
    batch, seq, _ = x.shape
    depth = w_in.shape[0]
    d_rwkv_in = 3 * W_RWKV + LORA_DECAY + LORA_ICLR + LORA_GATE
    d_gla_in = 2 * QK_GLA + 2 * W_GLA + LORA_GK
    row = lambda t: t.reshape(1, -1).astype(F32)

    ltri = _block_tri(TB_MIX)
    hi = jnp.arange(2 * LANES) // HD_RWKV
    ones_blocks = (hi[:, None] == hi[None, :]).astype(BF16)

    assert depth == 1, "single-layer trunk: the final norm is fused into the layer's last kernel"
    l = 0
    h = x.reshape(batch * seq, D_MODEL)

    w_r = _pad_cols(w_in[l][:, :d_rwkv_in], ZR_COLS).astype(BF16)
    w_g = _pad_cols(w_in[l][:, d_rwkv_in:d_rwkv_in + d_gla_in], ZG_COLS).astype(BF16)

    def lora_rows(w, start):
        return jnp.zeros((LORA_PAD, W_RWKV), F32).at[start:start + w.shape[0]].set(w).astype(BF16)

    du = lora_rows(decay_up[l], 0)
    iu = lora_rows(iclr_up[l], LORA_DECAY)
    gu = lora_rows(gate_up[l], LORA_DECAY + LORA_ICLR)
    mu = _pad_cols(row(mu_shift[l]), ZR_COLS)
    *prepared, wc, zg = _prep(h, row(rms1_g[l]), w_r, w_g, mu, row(decay_base[l]), du, row(iclr_base[l]), iu, gu,
                              row(k_k[l]), row(k_a[l]), row(r_k[l]), ones_blocks, ltri, seq)
    gku = jnp.zeros((GK_PAD, QK_GLA), F32).at[0:LORA_GK].set(gk_up[l]).astype(BF16)
    mix = _mixers(prepared, wc, zg, row(lnx_w[l]), row(lnx_b[l]), ones_blocks,
                  gku, row(gk_bias[l]), row(gla_norm_g[l]), ltri, batch, seq)

    out = _ffn(h, mix.reshape(batch * seq, D_MODEL), w_out[l].astype(BF16), row(rms2_g[l]),
               ffn_gate[l].astype(BF16), ffn_up[l].astype(BF16), ffn_down[l].astype(BF16), row(final_g))
    return out.reshape(batch, seq, D_MODEL)
```

```python
import functools
import math

import jax
import jax.numpy as jnp
from jax import lax
from jax.experimental import pallas as pl
from jax.experimental.pallas import tpu as pltpu

F32 = jnp.float32
BF16 = jnp.bfloat16

D_MODEL = 1024
W_RWKV = 512
HD_RWKV = 64
LORA_DECAY = 32
LORA_ICLR = 32
LORA_GATE = 96
GN_EPS = 64e-5
W_GLA = 512
DV_GLA = 128
H_GLA = 4
DK_GLA = 64
LORA_GK = 16
GK_NORMALIZER = 16.0
GLA_EPS = 1e-5
D_FF = 2816
RMS_EPS = 1e-6
CHUNK = 64

LANES = 128
PAIR_ROWS = 2 * CHUNK
LORA_PAD = 256
ZR_COLS = 3 * W_RWKV + LORA_PAD
GK_PAD = 128
ZG_COLS = 2 * H_GLA * DK_GLA + 2 * W_GLA + GK_PAD
QK_GLA = H_GLA * DK_GLA

TM_PROJ = 512
TB_MIX = 256
VMEM_LIMIT = 56 * 1024 * 1024


def _dot(a, b):
    return jnp.dot(a, b, preferred_element_type=F32)


def _dot_nt(a, b):
    return lax.dot_general(a, b, (((1,), (1,)), ((), ())), preferred_element_type=F32)


def _dot_tn(a, b):
    return lax.dot_general(a, b, (((0,), (0,)), ((), ())), preferred_element_type=F32)


def _dot_01_rhs(m01, x):
    hi = x.astype(BF16)
    lo = (x - hi.astype(F32)).astype(BF16)
    return _dot(m01, hi) + _dot(m01, lo)


def _head_sums(x, ones_blocks):
    xb = x.astype(BF16)
    width = ones_blocks.shape[0]
    return jnp.concatenate([_dot(xb[:, c:c + width], ones_blocks) for c in range(0, x.shape[1], width)], axis=1)


def _sigmoid(x):
    return 1.0 / (1.0 + jnp.exp(-x))


def _softplus(x):
    return jnp.maximum(x, 0.0) + jnp.log(1.0 + jnp.exp(-jnp.abs(x)))


def _stack_heads(x, lane_is_h0):
    zero = jnp.zeros_like(x)
    return jnp.concatenate([jnp.where(lane_is_h0, x, zero), jnp.where(lane_is_h0, zero, x)], axis=0)


def _prep_kernel(x_ref, g1_ref, wr_ref, wg_ref, mu_ref, db_ref, du_ref, ib_ref, iu_ref, gu_ref, kk_ref, ka_ref,
                 rk_ref, ones_ref, ltri_ref,
                 at_ref, rt_ref, bc_ref, kc_ref, bh_ref, kh_ref, v_ref, gate_ref, bonus_ref, wc_ref, zg_ref,
                 zprev_ref, *, tiles_per_seq):
    tm = x_ref.shape[0]
    half = ltri_ref.shape[0]

    @pl.when(pl.program_id(0) % tiles_per_seq == 0)
    def _():
        zprev_ref[...] = jnp.zeros_like(zprev_ref)

    x = x_ref[...]
    n = x * lax.rsqrt(jnp.mean(x * x, axis=-1, keepdims=True) + RMS_EPS) * g1_ref[...]
    n = n.astype(BF16)
    ones_blocks = ones_ref[...]
    row = lax.broadcasted_iota(jnp.int32, (half, 1), 0)
    cols_of = {"r": (0, W_RWKV), "k": (W_RWKV, 2 * W_RWKV), "v": (2 * W_RWKV, 3 * W_RWKV),
               "lo": (3 * W_RWKV, ZR_COLS)}
    z = {}

    def project(name):
        c0, c1 = cols_of[name]
        z[name] = _dot(n, wr_ref[:, c0:c1])

    def gla_columns(c):
        w = min(2 * LANES, ZG_COLS - c)
        zg_ref[:, c:c + w] = _dot(n, wg_ref[:, c:c + w])

    def shifted(name, r0):
        c0, c1 = cols_of[name]
        cur = z[name][r0:r0 + half]
        carry = zprev_ref[:, c0:c1] if r0 == 0 else z[name][r0 - 1:r0]
        prev = jnp.where(row == 0, carry, pltpu.roll(cur, 1, 0))
        return cur + (prev - cur) * mu_ref[:, c0:c1]

    keep = [{} for _ in range(tm // half)]

    def decay_part(h):
        r0, rows, st = h * half, slice(h * half, (h + 1) * half), keep[h]
        lo = shifted("lo", r0)
        xw = db_ref[...] + _dot(jnp.tanh(lo).astype(BF16), du_ref[...])
        ld = _sigmoid(xw) * (-math.exp(-0.5))
        st["a"] = _sigmoid(ib_ref[...] + _dot(lo.astype(BF16), iu_ref[...]))
        gate_ref[rows, :] = _dot(_sigmoid(lo).astype(BF16), gu_ref[...])
        cw = _dot_01_rhs(ltri_ref[...], ld)
        st["e_cw"] = jnp.exp(cw)
        st["e_neg"] = jnp.exp(-cw)
        st["e_prev"] = jnp.exp(cw - ld)
        for c0 in range(0, half, CHUNK):
            wc_ref[(r0 + c0) // CHUNK:(r0 + c0) // CHUNK + 1, :] = st["e_cw"][c0 + CHUNK - 1:c0 + CHUNK, :]

    def key_part(h):
        r0, rows, st = h * half, slice(h * half, (h + 1) * half), keep[h]
        k = shifted("k", r0)
        kk = k * kk_ref[...]
        kk = kk * lax.rsqrt(jnp.maximum(_head_sums(kk * kk, ones_blocks), 1e-24))
        st["k2"] = k2 = k * (1.0 + (st["a"] - 1.0) * ka_ref[...])
        at_ref[rows, :] = (-kk * st["e_prev"]).astype(BF16)
        bnc = kk * st["a"] * st["e_neg"]
        knc = k2 * st["e_neg"]
        bc_ref[rows, :] = bnc.astype(BF16)
        kc_ref[rows, :] = knc.astype(BF16)
        for c0 in range(0, half, CHUNK):
            w_c = st["e_cw"][c0 + CHUNK - 1:c0 + CHUNK, :]
            crows = slice(r0 + c0, r0 + c0 + CHUNK)
            bh_ref[crows, :] = (bnc[c0:c0 + CHUNK] * w_c).astype(BF16)
            kh_ref[crows, :] = (knc[c0:c0 + CHUNK] * w_c).astype(BF16)

    def receptance_part(h):
        rows, st = slice(h * half, (h + 1) * half), keep[h]
        r = shifted("r", h * half)
        rt_ref[rows, :] = (r * st["e_cw"]).astype(BF16)
        st["rk"] = _head_sums(r * st["k2"] * rk_ref[...], ones_blocks)

    def value_part(h):
        rows, st = slice(h * half, (h + 1) * half), keep[h]
        v = shifted("v", h * half)
        v_ref[rows, :] = v.astype(BF16)
        bonus_ref[rows, :] = st["rk"] * v

    halves = range(tm // half)
    token_work = ([functools.partial(decay_part, h) for h in halves] + [functools.partial(key_part, h) for h in halves]
                  + [functools.partial(receptance_part, h) for h in halves]
                  + [functools.partial(value_part, h) for h in halves])
    projections = ([functools.partial(project, name) for name in ("lo", "k", "r", "v")]
                   + [functools.partial(gla_columns, c) for c in range(0, ZG_COLS, 2 * LANES)])
    projections.pop(0)()
    projections.pop(0)()
    while token_work or projections:
        if token_work:
            token_work.pop(0)()
        if projections:
            projections.pop(0)()
    for name, (c0, c1) in cols_of.items():
        zprev_ref[:, c0:c1] = z[name][tm - 1:tm, :]


def _const_spec(shape):
    return pl.BlockSpec(shape, lambda *_: (0,) * len(shape), pipeline_mode=pl.Buffered(1))


def _prep(x2d, g1, w_r, w_g, mu, db, du, ib, iu, gu, k_k, k_a, r_k, ones_blocks, ltri, seq):
    n_tok = x2d.shape[0]
    tm = TM_PROJ
    row512 = _const_spec((1, W_RWKV))
    tile = lambda width: pl.BlockSpec((tm, width), lambda i: (i, 0))
    bf_out = jax.ShapeDtypeStruct((n_tok, W_RWKV), BF16)
    f32_out = jax.ShapeDtypeStruct((n_tok, W_RWKV), F32)
    return pl.pallas_call(
        functools.partial(_prep_kernel, tiles_per_seq=seq // tm),
        grid=(n_tok // tm,),
        in_specs=[
            tile(D_MODEL),
            _const_spec((1, D_MODEL)),
            _const_spec((D_MODEL, ZR_COLS)),
            _const_spec((D_MODEL, ZG_COLS)),
            _const_spec((1, ZR_COLS)),
            row512, _const_spec((LORA_PAD, W_RWKV)),
            row512, _const_spec((LORA_PAD, W_RWKV)),
            _const_spec((LORA_PAD, W_RWKV)),
            row512, row512, row512,
            _const_spec((2 * LANES, 2 * LANES)),
            _const_spec((TB_MIX, TB_MIX)),
        ],
        out_specs=[tile(W_RWKV)] * 9 + [
            pl.BlockSpec((tm // CHUNK, W_RWKV), lambda i: (i, 0)),
            tile(ZG_COLS),
        ],
        out_shape=[bf_out] * 7 + [f32_out] * 2 + [
            jax.ShapeDtypeStruct((n_tok // CHUNK, W_RWKV), F32),
            jax.ShapeDtypeStruct((n_tok, ZG_COLS), F32),
        ],
        scratch_shapes=[pltpu.VMEM((1, ZR_COLS), F32)],
        compiler_params=pltpu.CompilerParams(
            dimension_semantics=("arbitrary",), vmem_limit_bytes=VMEM_LIMIT),
        name="inproj_prep",
    )(x2d, g1, w_r, w_g, mu, db, du, ib, iu, gu, k_k, k_a, r_k, ones_blocks, ltri)


def _rwkv_steps(at_ref, rt_ref, bc_ref, kc_ref, bh_ref, kh_ref, v_ref, gate_ref, bonus_ref, wc_ref,
                lw_ref, lb_ref, ones_ref, y_ref, s_ref, ybuf_ref):
    tb = at_ref.shape[0]
    n_chunks = tb // CHUNK
    n_pairs = W_RWKV // LANES
    j = pl.program_id(1)

    lane_is_h0 = lax.broadcasted_iota(jnp.int32, (CHUNK, LANES), 1) < HD_RWKV
    ti = lax.broadcasted_iota(jnp.int32, (CHUNK, LANES), 0)
    tj = lax.broadcasted_iota(jnp.int32, (CHUNK, LANES), 1) % CHUNK
    strict_lower = ti > tj
    incl_lower = ti >= tj
    eye = (ti == tj).astype(F32)
    ri = lax.broadcasted_iota(jnp.int32, (PAIR_ROWS, LANES), 0)
    ci = lax.broadcasted_iota(jnp.int32, (PAIR_ROWS, LANES), 1)
    same_head = (ri < HD_RWKV) == (ci < HD_RWKV)
    stack = lambda x: _stack_heads(x, lane_is_h0)
    bf = lambda x: x.astype(BF16)

    units = [(n, p) for n in range(n_chunks) for p in range(n_pairs)]
    rows_of = lambda n: pl.ds(n * CHUNK, CHUNK)
    lanes_of = lambda p: pl.ds(p * LANES, LANES)
    load = lambda ref: [ref[rows_of(n), lanes_of(p)] for n, p in units]
    w = {}

    def cross_products():
        for name, ref in (("at", at_ref), ("rt", rt_ref), ("bc", bc_ref), ("kc", kc_ref), ("v", v_ref),
                          ("bh", bh_ref), ("kh", kh_ref)):
            w[name] = load(ref)
        w["x4"] = [_dot_nt(jnp.concatenate([a_, r_], axis=0), jnp.concatenate([stack(b_), stack(k_)], axis=0))
                   for a_, r_, b_, k_ in zip(w["at"], w["rt"], w["bc"], w["kc"])]

    def masks():
        x4 = w.pop("x4")
        w["a_ab"] = [jnp.where(strict_lower, x[0:CHUNK, 0:LANES], 0.0) for x in x4]
        w["a_k"] = [bf(jnp.concatenate([jnp.where(strict_lower, x[0:CHUNK, LANES:], 0.0),
                                        jnp.where(incl_lower, x[CHUNK:, LANES:], 0.0)], axis=0)) for x in x4]
        w["a_rb"] = [bf(jnp.where(incl_lower, x[CHUNK:, 0:LANES], 0.0)) for x in x4]

    def values_products():
        w["av_both"] = [_dot(a_, stack(v_)) for a_, v_ in zip(w.pop("a_k"), w["v"])]

    def power_2():
        a_ab = w.pop("a_ab")
        w["t_inv"] = [eye + a_ for a_ in a_ab]
        w["a_pow"] = [bf(_dot(a_, stack(a_))) for a_ in map(bf, a_ab)]

    def next_power():
        x2 = [_dot(a_, jnp.concatenate([stack(a_), stack(bf(t_))], axis=1)) for a_, t_ in zip(w["a_pow"], w["t_inv"])]
        w["t_inv"] = [t_ + x[:, LANES:] for t_, x in zip(w["t_inv"], x2)]
        w["a_pow"] = [bf(x[:, 0:LANES]) for x in x2]

    def inverse_applied():
        t_inv = [t_ + _dot(a_, stack(bf(t_))) for t_, a_ in zip(w.pop("t_inv"), w.pop("a_pow"))]
        pq = [_dot(bf(t_), jnp.concatenate([stack(a_), stack(bf(avb[0:CHUNK]))], axis=1))
              for t_, a_, avb in zip(t_inv, w["at"], w["av_both"])]
        w["p_m"] = [bf(x[:, 0:LANES]) for x in pq]
        w["q_m"] = [x[:, LANES:] for x in pq]

    def transitions():
        w["m_p"] = [bf(jnp.where(same_head, _dot_tn(p_, b_), 0.0)) for p_, b_ in zip(w["p_m"], w["bh"])]
        w["n_p"] = [jnp.where(same_head, _dot_tn(jnp.concatenate([bf(q_), v_], axis=0),
                                                 jnp.concatenate([b_, k_], axis=0)), 0.0)
                    for q_, v_, b_, k_ in zip(w["q_m"], w["v"], w["bh"], w["kh"])]
        w["state"] = [s_ref[p] for p in range(n_pairs)]

    def chunk_step(n):
        ids = [n * n_pairs + p for p in range(n_pairs)]
        decay = wc_ref[pl.ds(j * n_chunks + n, 1), :]
        s_bf = [bf(s) for s in w["state"]]
        w["state"] = [s * decay[:, p * LANES:(p + 1) * LANES] + _dot(sb, w["m_p"][i]) + w["n_p"][i]
                      for p, (i, s, sb) in enumerate(zip(ids, w["state"], s_bf))]
        uy = [_dot_nt(jnp.concatenate([w["p_m"][i], w["rt"][i]], axis=0), sb) for i, sb in zip(ids, s_bf)]
        u = [x[0:CHUNK] + w["q_m"][i] for i, x in zip(ids, uy)]
        y_c = [x[CHUNK:] + _dot(w["a_rb"][i], stack(bf(u_))) + w["av_both"][i][CHUNK:]
               for i, x, u_ in zip(ids, uy, u)]
        for p in range(n_pairs):
            ybuf_ref[rows_of(n), lanes_of(p)] = y_c[p]

    def finish():
        for p in range(n_pairs):
            s_ref[p] = w["state"][p]
        ones_blocks = ones_ref[...]
        y = ybuf_ref[...]
        inv_hd = 1.0 / HD_RWKV
        mean = _head_sums(y, ones_blocks) * inv_hd
        d = y - mean
        var = _head_sums(d * d, ones_blocks) * inv_hd
        yn = d * lax.rsqrt(var + GN_EPS) * lw_ref[...] + lb_ref[...]
        y_ref[:, 0:W_RWKV] = ((yn + bonus_ref[...]) * gate_ref[...]).astype(y_ref.dtype)

    first = [cross_products, masks, values_products, power_2] + [next_power] * 4 + [inverse_applied, transitions]
    return first, [functools.partial(chunk_step, n) for n in range(n_chunks)] + [finish]


def _gla_steps(z_ref, gu_ref, gb_ref, ng_ref, ltri_ref, y_ref, s_ref):
    tb = z_ref.shape[0]
    n_chunks = tb // CHUNK
    n_pairs = QK_GLA // LANES
    units = [(n, p) for n in range(n_chunks) for p in range(n_pairs)]
    lane_is_h0 = lax.broadcasted_iota(jnp.int32, (CHUNK, LANES), 1) < DK_GLA
    ri = lax.broadcasted_iota(jnp.int32, (PAIR_ROWS, CHUNK), 0)
    ci = lax.broadcasted_iota(jnp.int32, (PAIR_ROWS, CHUNK), 1)
    causal = jnp.where(ri < CHUNK, ri, ri - CHUNK) >= ci
    stack = lambda x: _stack_heads(x, lane_is_h0)
    w = {}

    def gate_logits():
        lo = z_ref[:, 2 * QK_GLA + 2 * W_GLA:ZG_COLS]
        x = _dot(lo.astype(BF16), gu_ref[...]) + gb_ref[...]
        w["gk"] = -_softplus(-x) * (1.0 / GK_NORMALIZER)

    def cumulative_gates():
        b = _dot_01_rhs(ltri_ref[...], w.pop("gk"))
        w["e_b"] = jnp.exp(b)
        w["e_nb"] = jnp.exp(-b)

    def scaled_operands():
        e_b, e_nb = w.pop("e_b"), w.pop("e_nb")
        q_in = (z_ref[:, 0:QK_GLA] * (DK_GLA ** -0.5) * e_b).astype(BF16)
        k_in = z_ref[:, QK_GLA:2 * QK_GLA] * e_nb
        k_in_bf = k_in.astype(BF16)
        v = z_ref[:, 2 * QK_GLA:2 * QK_GLA + W_GLA].astype(BF16)
        chunk = lambda t, n, c0, width: t[n * CHUNK:(n + 1) * CHUNK, c0:c0 + width]
        w["decay"] = [e_b[(n + 1) * CHUNK - 1:(n + 1) * CHUNK, :] for n in range(n_chunks)]
        w["q"] = [stack(chunk(q_in, n, p * LANES, LANES)) for n, p in units]
        w["k"] = [chunk(k_in_bf, n, p * LANES, LANES) for n, p in units]
        w["k_end"] = [stack((chunk(k_in, n, p * LANES, LANES)
                             * w["decay"][n][:, p * LANES:(p + 1) * LANES]).astype(BF16)) for n, p in units]
        w["v0"] = [chunk(v, n, (2 * p) * DV_GLA, DV_GLA) for n, p in units]
        w["v1"] = [chunk(v, n, (2 * p + 1) * DV_GLA, DV_GLA) for n, p in units]

    def scores():
        w["att"] = [jnp.where(causal, _dot_nt(q_, k_), 0.0).astype(BF16) for q_, k_ in zip(w["q"], w["k"])]

    def state_updates():
        ds = [_dot_tn(jnp.concatenate([v0, v1], axis=0), ke) for v0, v1, ke in zip(w["v0"], w["v1"], w["k_end"])]
        states = [s_ref[p] for p in range(n_pairs)]
        w["s_in"] = []
        for n in range(n_chunks):
            w["s_in"] += states
            states = [s * w["decay"][n][:, p * LANES:(p + 1) * LANES] + ds[n * n_pairs + p]
                      for p, s in enumerate(states)]
        for p in range(n_pairs):
            s_ref[p] = states[p]

    def outputs():
        intra0 = [_dot(a_[0:CHUNK], v_) for a_, v_ in zip(w["att"], w["v0"])]
        intra1 = [_dot(a_[CHUNK:], v_) for a_, v_ in zip(w["att"], w["v1"])]
        inter = [_dot_nt(q_, s_.astype(BF16)) for q_, s_ in zip(w["q"], w["s_in"])]
        w["o"] = {}
        for i, (n, p) in enumerate(units):
            w["o"][(n, 2 * p)] = intra0[i] + inter[i][0:CHUNK]
            w["o"][(n, 2 * p + 1)] = intra1[i] + inter[i][CHUNK:]

    def finish():
        ng = ng_ref[...]
        for h in range(H_GLA):
            o = jnp.concatenate([w["o"][(n, h)] for n in range(n_chunks)], axis=0)
            og = z_ref[:, pl.ds(2 * QK_GLA + W_GLA + h * DV_GLA, DV_GLA)]
            o = o * lax.rsqrt(jnp.mean(o * o, axis=-1, keepdims=True) + GLA_EPS) * ng
            y_ref[:, pl.ds(W_RWKV + h * DV_GLA, DV_GLA)] = (o * (og * _sigmoid(og))).astype(y_ref.dtype)

    return [gate_logits, cumulative_gates, scaled_operands], [scores, state_updates, outputs, finish]


def _mixer_kernel(at_ref, rt_ref, bc_ref, kc_ref, bh_ref, kh_ref, v_ref, gate_ref, bonus_ref, wc_ref,
                  lw_ref, lb_ref, ones_ref, zg_ref, gu_ref, gb_ref, ng_ref, ltri_ref,
                  y_ref,
                  s_rwkv_ref, ybuf_ref, s_gla_ref):
    @pl.when(pl.program_id(1) == 0)
    def _():
        s_rwkv_ref[...] = jnp.zeros_like(s_rwkv_ref)
        s_gla_ref[...] = jnp.zeros_like(s_gla_ref)

    rwkv_first, rwkv_chain = _rwkv_steps(at_ref, rt_ref, bc_ref, kc_ref, bh_ref, kh_ref, v_ref, gate_ref, bonus_ref,
                                         wc_ref, lw_ref, lb_ref, ones_ref, y_ref, s_rwkv_ref, ybuf_ref)
    gla_tokens, gla_chunks = _gla_steps(zg_ref, gu_ref, gb_ref, ng_ref, ltri_ref, y_ref, s_gla_ref)
    for group_a, group_b in ((rwkv_first, gla_tokens), (rwkv_chain, gla_chunks)):
        group_a, group_b = list(group_a), list(group_b)
        while group_a or group_b:
            if group_a:
                group_a.pop(0)()
            if group_b:
                group_b.pop(0)()


def _mixers(prepared, wc, zg, lw, lb, ones_blocks, gk_up, gk_bias, norm_g, ltri, batch, seq):
    tb = TB_MIX
    row512 = _const_spec((1, W_RWKV))
    slab = pl.BlockSpec((None, tb, W_RWKV), lambda b, j: (b, j, 0))
    prepared = [t.reshape(batch, seq, W_RWKV) for t in prepared]
    return pl.pallas_call(
        _mixer_kernel,
        grid=(batch, seq // tb),
        in_specs=[slab] * 9 + [
            pl.BlockSpec((None, seq // CHUNK, W_RWKV), lambda b, j: (b, 0, 0)),
            row512, row512,
            _const_spec((2 * LANES, 2 * LANES)),
            pl.BlockSpec((None, tb, ZG_COLS), lambda b, j: (b, j, 0)),
            _const_spec((GK_PAD, QK_GLA)),
            _const_spec((1, QK_GLA)),
            _const_spec((1, DV_GLA)),
            _const_spec((tb, tb)),
        ],
        out_specs=pl.BlockSpec((None, tb, D_MODEL), lambda b, j: (b, j, 0)),
        out_shape=jax.ShapeDtypeStruct((batch, seq, D_MODEL), BF16),
        scratch_shapes=[
            pltpu.VMEM((W_RWKV // LANES, LANES, LANES), F32),
            pltpu.VMEM((tb, W_RWKV), F32),
            pltpu.VMEM((QK_GLA // LANES, DV_GLA, LANES), F32),
        ],
        compiler_params=pltpu.CompilerParams(
            dimension_semantics=("parallel", "arbitrary"), vmem_limit_bytes=VMEM_LIMIT),
        name="mixers",
    )(*prepared, wc.reshape(batch, seq // CHUNK, W_RWKV), lw, lb, ones_blocks,
      zg.reshape(batch, seq, ZG_COLS), gk_up, gk_bias, norm_g, ltri)


def _ffn_kernel(x_ref, mix_ref, wo_ref, g2_ref, wg_ref, wu_ref, wd_ref, gf_ref, o_ref):
    h = x_ref[...] + _dot(mix_ref[...], wo_ref[...])
    n = h * lax.rsqrt(jnp.mean(h * h, axis=-1, keepdims=True) + RMS_EPS) * g2_ref[...]
    n = n.astype(BF16)
    gate = _dot(n, wg_ref[...])
    up = _dot(n, wu_ref[...])
    act = (gate * _sigmoid(gate) * up).astype(BF16)
    h = h + _dot(act, wd_ref[...])
    o_ref[...] = h * lax.rsqrt(jnp.mean(h * h, axis=-1, keepdims=True) + RMS_EPS) * gf_ref[...]


def _ffn(x2d, mix, wo, g2, wg, wu, wd, gf):
    n_tok = x2d.shape[0]
    tm = TM_PROJ
    return pl.pallas_call(
        _ffn_kernel,
        grid=(n_tok // tm,),
        in_specs=[
            pl.BlockSpec((tm, D_MODEL), lambda i: (i, 0)),
            pl.BlockSpec((tm, D_MODEL), lambda i: (i, 0)),
            _const_spec((D_MODEL, D_MODEL)),
            _const_spec((1, D_MODEL)),
            _const_spec((D_MODEL, D_FF)),
            _const_spec((D_MODEL, D_FF)),
            _const_spec((D_FF, D_MODEL)),
            _const_spec((1, D_MODEL)),
        ],
        out_specs=pl.BlockSpec((tm, D_MODEL), lambda i: (i, 0)),
        out_shape=jax.ShapeDtypeStruct((n_tok, D_MODEL), F32),
        compiler_params=pltpu.CompilerParams(
            dimension_semantics=("parallel",), vmem_limit_bytes=VMEM_LIMIT),
        name="outproj_ffn",
    )(x2d, mix, wo, g2, wg, wu, wd, gf)


def _pad_cols(w, width):
    return jnp.pad(w, ((0, 0), (0, width - w.shape[1])))


def _block_tri(tb):
    i = jnp.arange(tb)
    same = (i[:, None] // CHUNK) == (i[None, :] // CHUNK)
    return (same & (i[:, None] >= i[None, :])).astype(BF16)


def kernel(x, rms1_g, w_in, mu_shift, decay_base, decay_up, iclr_base, iclr_up, gate_up, k_k, k_a, r_k,
           lnx_w, lnx_b, gk_up, gk_bias, gla_norm_g, w_out, rms2_g, ffn_gate, ffn_up, ffn_down, final_g):
    batch, seq, _ = x.shape
    depth = w_in.shape[0]
    d_rwkv_in = 3 * W_RWKV + LORA_DECAY + LORA_ICLR + LORA_GATE
    d_gla_in = 2 * QK_GLA + 2 * W_GLA + LORA_GK
    row = lambda t: t.reshape(1, -1).astype(F32)

    ltri = _block_tri(TB_MIX)
    hi = jnp.arange(2 * LANES) // HD_RWKV
    ones_blocks = (hi[:, None] == hi[None, :]).astype(BF16)

    assert depth == 1, "single-layer trunk: the final norm is fused into the layer's last kernel"
    l = 0
    h = x.reshape(batch * seq, D_MODEL)

    w_r = _pad_cols(w_in[l][:, :d_rwkv_in], ZR_COLS).astype(BF16)
    w_g = _pad_cols(w_in[l][:, d_rwkv_in:d_rwkv_in + d_gla_in], ZG_COLS).astype(BF16)

    def lora_rows(w, start):
        return jnp.zeros((LORA_PAD, W_RWKV), F32).at[start:start + w.shape[0]].set(w).astype(BF16)

    du = lora_rows(decay_up[l], 0)
    iu = lora_rows(iclr_up[l], LORA_DECAY)
    gu = lora_rows(gate_up[l], LORA_DECAY + LORA_ICLR)
    mu = _pad_cols(row(mu_shift[l]), ZR_COLS)
    *prepared, wc, zg = _prep(h, row(rms1_g[l]), w_r, w_g, mu, row(decay_base[l]), du, row(iclr_base[l]), iu, gu,
                              row(k_k[l]), row(k_a[l]), row(r_k[l]), ones_blocks, ltri, seq)
    gku = jnp.zeros((GK_PAD, QK_GLA), F32).at[0:LORA_GK].set(gk_up[l]).astype(BF16)
    mix = _mixers(prepared, wc, zg, row(lnx_w[l]), row(lnx_b[l]), ones_blocks,
                  gku, row(gk_bias[l]), row(gla_norm_g[l]), ltri, batch, seq)

    out = _ffn(h, mix.reshape(batch * seq, D_MODEL), w_out[l].astype(BF16), row(rms2_g[l]),
               ffn_gate[l].astype(BF16), ffn_up[l].astype(BF16), ffn_down[l].astype(BF16), row(final_g))
    return out.reshape(batch, seq, D_MODEL)
```

```python
import functools
import math

import jax
import jax.numpy as jnp
from jax import lax
from jax.experimental import pallas as pl
from jax.experimental.pallas import tpu as pltpu

F32 = jnp.float32
BF16 = jnp.bfloat16

D_MODEL = 1024
W_RWKV = 512
HD_RWKV = 64
LORA_DECAY = 32
LORA_ICLR = 32
LORA_GATE = 96
GN_EPS = 64e-5
W_GLA = 512
DV_GLA = 128
H_GLA = 4
DK_GLA = 64
LORA_GK = 16
GK_NORMALIZER = 16.0
GLA_EPS = 1e-5
D_FF = 2816
RMS_EPS = 1e-6
CHUNK = 64

LANES = 128
PAIR_ROWS = 2 * CHUNK
LORA_PAD = 256
ZR_COLS = 3 * W_RWKV + LORA_PAD
GK_PAD = 128
ZG_COLS = 2 * H_GLA * DK_GLA + 2 * W_GLA + GK_PAD
QK_GLA = H_GLA * DK_GLA

TM_PROJ = 512
TB_MIX = 512
CHUNKS_PER_GROUP = 4
CUMSUM_ROWS = 256
VMEM_LIMIT = 56 * 1024 * 1024


def _dot(a, b):
    return jnp.dot(a, b, preferred_element_type=F32)


def _dot_nt(a, b):
    return lax.dot_general(a, b, (((1,), (1,)), ((), ())), preferred_element_type=F32)


def _dot_tn(a, b):
    return lax.dot_general(a, b, (((0,), (0,)), ((), ())), preferred_element_type=F32)


def _dot_01_rhs(m01, x):
    hi = x.astype(BF16)
    lo = (x - hi.astype(F32)).astype(BF16)
    return _dot(m01, hi) + _dot(m01, lo)


def _head_sums(x, ones_blocks):
    xb = x.astype(BF16)
    width = ones_blocks.shape[0]
    return jnp.concatenate([_dot(xb[:, c:c + width], ones_blocks) for c in range(0, x.shape[1], width)], axis=1)


def _sigmoid(x):
    return 1.0 / (1.0 + jnp.exp(-x))


def _softplus(x):
    return jnp.maximum(x, 0.0) + jnp.log(1.0 + jnp.exp(-jnp.abs(x)))


def _stack_heads(x, lane_is_h0):
    zero = jnp.zeros_like(x)
    return jnp.concatenate([jnp.where(lane_is_h0, x, zero), jnp.where(lane_is_h0, zero, x)], axis=0)


def _prep_kernel(x_ref, g1_ref, wr_ref, wg_ref, mu_ref, db_ref, du_ref, ib_ref, iu_ref, gu_ref, kk_ref, ka_ref,
                 rk_ref, ones_ref, ltri_ref,
                 at_ref, rt_ref, bc_ref, kc_ref, bh_ref, kh_ref, v_ref, gate_ref, bonus_ref, wc_ref, zg_ref,
                 zprev_ref, *, tiles_per_seq):
    tm = x_ref.shape[0]
    half = ltri_ref.shape[0]

    @pl.when(pl.program_id(0) % tiles_per_seq == 0)
    def _():
        zprev_ref[...] = jnp.zeros_like(zprev_ref)

    x = x_ref[...]
    n = x * lax.rsqrt(jnp.mean(x * x, axis=-1, keepdims=True) + RMS_EPS) * g1_ref[...]
    n = n.astype(BF16)
    ones_blocks = ones_ref[...]
    row = lax.broadcasted_iota(jnp.int32, (half, 1), 0)
    cols_of = {"r": (0, W_RWKV), "k": (W_RWKV, 2 * W_RWKV), "v": (2 * W_RWKV, 3 * W_RWKV),
               "lo": (3 * W_RWKV, ZR_COLS)}
    z = {}

    def project(name):
        c0, c1 = cols_of[name]
        z[name] = _dot(n, wr_ref[:, c0:c1])

    def gla_columns(c):
        w = min(2 * LANES, ZG_COLS - c)
        zg_ref[:, c:c + w] = _dot(n, wg_ref[:, c:c + w])

    def shifted(name, r0):
        c0, c1 = cols_of[name]
        cur = z[name][r0:r0 + half]
        carry = zprev_ref[:, c0:c1] if r0 == 0 else z[name][r0 - 1:r0]
        prev = jnp.where(row == 0, carry, pltpu.roll(cur, 1, 0))
        return cur + (prev - cur) * mu_ref[:, c0:c1]

    keep = [{} for _ in range(tm // half)]

    def decay_part(h):
        r0, rows, st = h * half, slice(h * half, (h + 1) * half), keep[h]
        lo = shifted("lo", r0)
        xw = db_ref[...] + _dot(jnp.tanh(lo).astype(BF16), du_ref[...])
        ld = _sigmoid(xw) * (-math.exp(-0.5))
        st["a"] = _sigmoid(ib_ref[...] + _dot(lo.astype(BF16), iu_ref[...]))
        gate_ref[rows, :] = _dot(_sigmoid(lo).astype(BF16), gu_ref[...])
        cw = _dot_01_rhs(ltri_ref[...], ld)
        st["e_cw"] = jnp.exp(cw)
        st["e_neg"] = jnp.exp(-cw)
        st["e_prev"] = jnp.exp(cw - ld)
        for c0 in range(0, half, CHUNK):
            wc_ref[(r0 + c0) // CHUNK:(r0 + c0) // CHUNK + 1, :] = st["e_cw"][c0 + CHUNK - 1:c0 + CHUNK, :]

    def key_part(h):
        r0, rows, st = h * half, slice(h * half, (h + 1) * half), keep[h]
        k = shifted("k", r0)
        kk = k * kk_ref[...]
        kk = kk * lax.rsqrt(jnp.maximum(_head_sums(kk * kk, ones_blocks), 1e-24))
        st["k2"] = k2 = k * (1.0 + (st["a"] - 1.0) * ka_ref[...])
        at_ref[rows, :] = (-kk * st["e_prev"]).astype(BF16)
        bnc = kk * st["a"] * st["e_neg"]
        knc = k2 * st["e_neg"]
        bc_ref[rows, :] = bnc.astype(BF16)
        kc_ref[rows, :] = knc.astype(BF16)
        for c0 in range(0, half, CHUNK):
            w_c = st["e_cw"][c0 + CHUNK - 1:c0 + CHUNK, :]
            crows = slice(r0 + c0, r0 + c0 + CHUNK)
            bh_ref[crows, :] = (bnc[c0:c0 + CHUNK] * w_c).astype(BF16)
            kh_ref[crows, :] = (knc[c0:c0 + CHUNK] * w_c).astype(BF16)

    def receptance_part(h):
        rows, st = slice(h * half, (h + 1) * half), keep[h]
        r = shifted("r", h * half)
        rt_ref[rows, :] = (r * st["e_cw"]).astype(BF16)
        st["rk"] = _head_sums(r * st["k2"] * rk_ref[...], ones_blocks)

    def value_part(h):
        rows, st = slice(h * half, (h + 1) * half), keep[h]
        v = shifted("v", h * half)
        v_ref[rows, :] = v.astype(BF16)
        bonus_ref[rows, :] = st["rk"] * v

    halves = range(tm // half)
    token_work = ([functools.partial(decay_part, h) for h in halves] + [functools.partial(key_part, h) for h in halves]
                  + [functools.partial(receptance_part, h) for h in halves]
                  + [functools.partial(value_part, h) for h in halves])
    projections = ([functools.partial(project, name) for name in ("lo", "k", "r", "v")]
                   + [functools.partial(gla_columns, c) for c in range(0, ZG_COLS, 2 * LANES)])
    projections.pop(0)()
    projections.pop(0)()
    while token_work or projections:
        if token_work:
            token_work.pop(0)()
        if projections:
            projections.pop(0)()
    for name, (c0, c1) in cols_of.items():
        zprev_ref[:, c0:c1] = z[name][tm - 1:tm, :]


def _const_spec(shape):
    return pl.BlockSpec(shape, lambda *_: (0,) * len(shape), pipeline_mode=pl.Buffered(1))


def _prep(x2d, g1, w_r, w_g, mu, db, du, ib, iu, gu, k_k, k_a, r_k, ones_blocks, ltri, seq):
    n_tok = x2d.shape[0]
    tm = TM_PROJ
    row512 = _const_spec((1, W_RWKV))
    tile = lambda width: pl.BlockSpec((tm, width), lambda i: (i, 0))
    bf_out = jax.ShapeDtypeStruct((n_tok, W_RWKV), BF16)
    f32_out = jax.ShapeDtypeStruct((n_tok, W_RWKV), F32)
    return pl.pallas_call(
        functools.partial(_prep_kernel, tiles_per_seq=seq // tm),
        grid=(n_tok // tm,),
        in_specs=[
            tile(D_MODEL),
            _const_spec((1, D_MODEL)),
            _const_spec((D_MODEL, ZR_COLS)),
            _const_spec((D_MODEL, ZG_COLS)),
            _const_spec((1, ZR_COLS)),
            row512, _const_spec((LORA_PAD, W_RWKV)),
            row512, _const_spec((LORA_PAD, W_RWKV)),
            _const_spec((LORA_PAD, W_RWKV)),
            row512, row512, row512,
            _const_spec((2 * LANES, 2 * LANES)),
            _const_spec((CUMSUM_ROWS, CUMSUM_ROWS)),
        ],
        out_specs=[tile(W_RWKV)] * 9 + [
            pl.BlockSpec((tm // CHUNK, W_RWKV), lambda i: (i, 0)),
            tile(ZG_COLS),
        ],
        out_shape=[bf_out] * 7 + [f32_out] * 2 + [
            jax.ShapeDtypeStruct((n_tok // CHUNK, W_RWKV), F32),
            jax.ShapeDtypeStruct((n_tok, ZG_COLS), F32),
        ],
        scratch_shapes=[pltpu.VMEM((1, ZR_COLS), F32)],
        compiler_params=pltpu.CompilerParams(
            dimension_semantics=("arbitrary",), vmem_limit_bytes=VMEM_LIMIT),
        name="inproj_prep",
    )(x2d, g1, w_r, w_g, mu, db, du, ib, iu, gu, k_k, k_a, r_k, ones_blocks, ltri)


def _rwkv_steps(chunk_ids, state, at_ref, rt_ref, bc_ref, kc_ref, bh_ref, kh_ref, v_ref, wc_ref, ybuf_ref):
    n_chunks = at_ref.shape[0] // CHUNK
    n_pairs = W_RWKV // LANES
    j = pl.program_id(1)

    lane_is_h0 = lax.broadcasted_iota(jnp.int32, (CHUNK, LANES), 1) < HD_RWKV
    ti = lax.broadcasted_iota(jnp.int32, (CHUNK, LANES), 0)
    tj = lax.broadcasted_iota(jnp.int32, (CHUNK, LANES), 1) % CHUNK
    strict_lower = ti > tj
    incl_lower = ti >= tj
    eye = (ti == tj).astype(F32)
    ri = lax.broadcasted_iota(jnp.int32, (PAIR_ROWS, LANES), 0)
    ci = lax.broadcasted_iota(jnp.int32, (PAIR_ROWS, LANES), 1)
    same_head = (ri < HD_RWKV) == (ci < HD_RWKV)
    stack = lambda x: _stack_heads(x, lane_is_h0)
    bf = lambda x: x.astype(BF16)

    units = [(n, p) for n in chunk_ids for p in range(n_pairs)]
    rows_of = lambda n: pl.ds(n * CHUNK, CHUNK)
    lanes_of = lambda p: pl.ds(p * LANES, LANES)
    load = lambda ref: [ref[rows_of(n), lanes_of(p)] for n, p in units]
    w = {}

    def cross_products():
        for name, ref in (("at", at_ref), ("rt", rt_ref), ("bc", bc_ref), ("kc", kc_ref), ("v", v_ref),
                          ("bh", bh_ref), ("kh", kh_ref)):
            w[name] = load(ref)
        w["x4"] = [_dot_nt(jnp.concatenate([a_, r_], axis=0), jnp.concatenate([stack(b_), stack(k_)], axis=0))
                   for a_, r_, b_, k_ in zip(w["at"], w["rt"], w["bc"], w["kc"])]

    def masks():
        x4 = w.pop("x4")
        w["a_ab"] = [jnp.where(strict_lower, x[0:CHUNK, 0:LANES], 0.0) for x in x4]
        w["a_k"] = [bf(jnp.concatenate([jnp.where(strict_lower, x[0:CHUNK, LANES:], 0.0),
                                        jnp.where(incl_lower, x[CHUNK:, LANES:], 0.0)], axis=0)) for x in x4]
        w["a_rb"] = [bf(jnp.where(incl_lower, x[CHUNK:, 0:LANES], 0.0)) for x in x4]

    def values_products():
        w["av_both"] = [_dot(a_, stack(v_)) for a_, v_ in zip(w.pop("a_k"), w["v"])]

    def power_2():
        a_ab = w.pop("a_ab")
        w["t_inv"] = [eye + a_ for a_ in a_ab]
        w["a_pow"] = [bf(_dot(a_, stack(a_))) for a_ in map(bf, a_ab)]

    def next_power():
        x2 = [_dot(a_, jnp.concatenate([stack(a_), stack(bf(t_))], axis=1)) for a_, t_ in zip(w["a_pow"], w["t_inv"])]
        w["t_inv"] = [t_ + x[:, LANES:] for t_, x in zip(w["t_inv"], x2)]
        w["a_pow"] = [bf(x[:, 0:LANES]) for x in x2]

    def inverse_applied():
        t_inv = [t_ + _dot(a_, stack(bf(t_))) for t_, a_ in zip(w.pop("t_inv"), w.pop("a_pow"))]
        pq = [_dot(bf(t_), jnp.concatenate([stack(a_), stack(bf(avb[0:CHUNK]))], axis=1))
              for t_, a_, avb in zip(t_inv, w["at"], w["av_both"])]
        w["p_m"] = [bf(x[:, 0:LANES]) for x in pq]
        w["q_m"] = [x[:, LANES:] for x in pq]

    def transitions():
        w["m_p"] = [bf(jnp.where(same_head, _dot_tn(p_, b_), 0.0)) for p_, b_ in zip(w["p_m"], w["bh"])]
        w["n_p"] = [jnp.where(same_head, _dot_tn(jnp.concatenate([bf(q_), v_], axis=0),
                                                 jnp.concatenate([b_, k_], axis=0)), 0.0)
                    for q_, v_, b_, k_ in zip(w["q_m"], w["v"], w["bh"], w["kh"])]

    def chunk_step(n):
        ids = [chunk_ids.index(n) * n_pairs + p for p in range(n_pairs)]
        decay = wc_ref[pl.ds(j * n_chunks + n, 1), :]
        s_bf = [bf(s) for s in state]
        state[:] = [s * decay[:, p * LANES:(p + 1) * LANES] + _dot(sb, w["m_p"][i]) + w["n_p"][i]
                    for p, (i, s, sb) in enumerate(zip(ids, state, s_bf))]
        uy = [_dot_nt(jnp.concatenate([w["p_m"][i], w["rt"][i]], axis=0), sb) for i, sb in zip(ids, s_bf)]
        u = [x[0:CHUNK] + w["q_m"][i] for i, x in zip(ids, uy)]
        y_c = [x[CHUNK:] + _dot(w["a_rb"][i], stack(bf(u_))) + w["av_both"][i][CHUNK:]
               for i, x, u_ in zip(ids, uy, u)]
        for p in range(n_pairs):
            ybuf_ref[rows_of(n), lanes_of(p)] = y_c[p]

    first = [cross_products, masks, values_products, power_2] + [next_power] * 4 + [inverse_applied, transitions]
    return first, [functools.partial(chunk_step, n) for n in chunk_ids]


def _rwkv_finish(ybuf_ref, gate_ref, bonus_ref, lw_ref, lb_ref, ones_ref, y_ref):
    ones_blocks = ones_ref[...]
    y = ybuf_ref[...]
    inv_hd = 1.0 / HD_RWKV
    mean = _head_sums(y, ones_blocks) * inv_hd
    d = y - mean
    var = _head_sums(d * d, ones_blocks) * inv_hd
    yn = d * lax.rsqrt(var + GN_EPS) * lw_ref[...] + lb_ref[...]
    y_ref[:, 0:W_RWKV] = ((yn + bonus_ref[...]) * gate_ref[...]).astype(y_ref.dtype)


def _gla_steps(z_ref, gu_ref, gb_ref, ng_ref, ltri_ref, y_ref, s_ref):
    tb = z_ref.shape[0]
    n_chunks = tb // CHUNK
    n_pairs = QK_GLA // LANES
    units = [(n, p) for n in range(n_chunks) for p in range(n_pairs)]
    lane_is_h0 = lax.broadcasted_iota(jnp.int32, (CHUNK, LANES), 1) < DK_GLA
    ri = lax.broadcasted_iota(jnp.int32, (PAIR_ROWS, CHUNK), 0)
    ci = lax.broadcasted_iota(jnp.int32, (PAIR_ROWS, CHUNK), 1)
    causal = jnp.where(ri < CHUNK, ri, ri - CHUNK) >= ci
    stack = lambda x: _stack_heads(x, lane_is_h0)
    w = {}

    def gate_logits():
        lo = z_ref[:, 2 * QK_GLA + 2 * W_GLA:ZG_COLS]
        x = _dot(lo.astype(BF16), gu_ref[...]) + gb_ref[...]
        w["gk"] = -_softplus(-x) * (1.0 / GK_NORMALIZER)

    def cumulative_gates():
        gk, span = w.pop("gk"), ltri_ref.shape[0]
        b = jnp.concatenate([_dot_01_rhs(ltri_ref[...], gk[r0:r0 + span]) for r0 in range(0, tb, span)],
                            axis=0)
        w["e_b"] = jnp.exp(b)
        w["e_nb"] = jnp.exp(-b)

    def scaled_operands():
        e_b, e_nb = w.pop("e_b"), w.pop("e_nb")
        q_in = (z_ref[:, 0:QK_GLA] * (DK_GLA ** -0.5) * e_b).astype(BF16)
        k_in = z_ref[:, QK_GLA:2 * QK_GLA] * e_nb
        k_in_bf = k_in.astype(BF16)
        v = z_ref[:, 2 * QK_GLA:2 * QK_GLA + W_GLA].astype(BF16)
        chunk = lambda t, n, c0, width: t[n * CHUNK:(n + 1) * CHUNK, c0:c0 + width]
        w["decay"] = [e_b[(n + 1) * CHUNK - 1:(n + 1) * CHUNK, :] for n in range(n_chunks)]
        w["q"] = [stack(chunk(q_in, n, p * LANES, LANES)) for n, p in units]
        w["k"] = [chunk(k_in_bf, n, p * LANES, LANES) for n, p in units]
        w["k_end"] = [stack((chunk(k_in, n, p * LANES, LANES)
                             * w["decay"][n][:, p * LANES:(p + 1) * LANES]).astype(BF16)) for n, p in units]
        w["v0"] = [chunk(v, n, (2 * p) * DV_GLA, DV_GLA) for n, p in units]
        w["v1"] = [chunk(v, n, (2 * p + 1) * DV_GLA, DV_GLA) for n, p in units]

    def scores():
        w["att"] = [jnp.where(causal, _dot_nt(q_, k_), 0.0).astype(BF16) for q_, k_ in zip(w["q"], w["k"])]

    def state_updates():
        ds = [_dot_tn(jnp.concatenate([v0, v1], axis=0), ke) for v0, v1, ke in zip(w["v0"], w["v1"], w["k_end"])]
        states = [s_ref[p] for p in range(n_pairs)]
        w["s_in"] = []
        for n in range(n_chunks):
            w["s_in"] += states
            states = [s * w["decay"][n][:, p * LANES:(p + 1) * LANES] + ds[n * n_pairs + p]
                      for p, s in enumerate(states)]
        for p in range(n_pairs):
            s_ref[p] = states[p]

    def outputs():
        intra0 = [_dot(a_[0:CHUNK], v_) for a_, v_ in zip(w["att"], w["v0"])]
        intra1 = [_dot(a_[CHUNK:], v_) for a_, v_ in zip(w["att"], w["v1"])]
        inter = [_dot_nt(q_, s_.astype(BF16)) for q_, s_ in zip(w["q"], w["s_in"])]
        w["o"] = {}
        for i, (n, p) in enumerate(units):
            w["o"][(n, 2 * p)] = intra0[i] + inter[i][0:CHUNK]
            w["o"][(n, 2 * p + 1)] = intra1[i] + inter[i][CHUNK:]

    def finish():
        ng = ng_ref[...]
        for h in range(H_GLA):
            o = jnp.concatenate([w["o"][(n, h)] for n in range(n_chunks)], axis=0)
            og = z_ref[:, pl.ds(2 * QK_GLA + W_GLA + h * DV_GLA, DV_GLA)]
            o = o * lax.rsqrt(jnp.mean(o * o, axis=-1, keepdims=True) + GLA_EPS) * ng
            y_ref[:, pl.ds(W_RWKV + h * DV_GLA, DV_GLA)] = (o * (og * _sigmoid(og))).astype(y_ref.dtype)

    return [gate_logits, cumulative_gates, scaled_operands], [scores, state_updates, outputs, finish]


def _mixer_kernel(at_ref, rt_ref, bc_ref, kc_ref, bh_ref, kh_ref, v_ref, gate_ref, bonus_ref, wc_ref,
                  lw_ref, lb_ref, ones_ref, zg_ref, gu_ref, gb_ref, ng_ref, ltri_ref,
                  y_ref,
                  s_rwkv_ref, ybuf_ref, s_gla_ref):
    @pl.when(pl.program_id(1) == 0)
    def _():
        s_rwkv_ref[...] = jnp.zeros_like(s_rwkv_ref)
        s_gla_ref[...] = jnp.zeros_like(s_gla_ref)

    n_chunks = at_ref.shape[0] // CHUNK
    n_pairs = W_RWKV // LANES
    state = [s_rwkv_ref[p] for p in range(n_pairs)]
    groups = [_rwkv_steps(list(range(c0, c0 + CHUNKS_PER_GROUP)), state, at_ref, rt_ref, bc_ref, kc_ref, bh_ref,
                          kh_ref, v_ref, wc_ref, ybuf_ref) for c0 in range(0, n_chunks, CHUNKS_PER_GROUP)]
    gla_tokens, gla_chunks = _gla_steps(zg_ref, gu_ref, gb_ref, ng_ref, ltri_ref, y_ref, s_gla_ref)

    main_work = [first for first, _ in groups] + [groups[-1][1]]
    side_work = [gla_tokens] + [chain for _, chain in groups[:-1]] + [[]]
    side_work[1] = side_work[1] + gla_chunks[:-1]
    side_work[-1] = side_work[-1] + gla_chunks[-1:]
    for group_a, group_b in zip(main_work, side_work):
        group_a, group_b = list(group_a), list(group_b)
        while group_a or group_b:
            if group_a:
                group_a.pop(0)()
            if group_b:
                group_b.pop(0)()
    for p in range(n_pairs):
        s_rwkv_ref[p] = state[p]
    _rwkv_finish(ybuf_ref, gate_ref, bonus_ref, lw_ref, lb_ref, ones_ref, y_ref)


def _mixers(prepared, wc, zg, lw, lb, ones_blocks, gk_up, gk_bias, norm_g, ltri, batch, seq):
    tb = TB_MIX
    row512 = _const_spec((1, W_RWKV))
    slab = pl.BlockSpec((None, tb, W_RWKV), lambda b, j: (b, j, 0))
    prepared = [t.reshape(batch, seq, W_RWKV) for t in prepared]
    return pl.pallas_call(
        _mixer_kernel,
        grid=(batch, seq // tb),
        in_specs=[slab] * 9 + [
            pl.BlockSpec((None, seq // CHUNK, W_RWKV), lambda b, j: (b, 0, 0)),
            row512, row512,
            _const_spec((2 * LANES, 2 * LANES)),
            pl.BlockSpec((None, tb, ZG_COLS), lambda b, j: (b, j, 0)),
            _const_spec((GK_PAD, QK_GLA)),
            _const_spec((1, QK_GLA)),
            _const_spec((1, DV_GLA)),
            _const_spec((CUMSUM_ROWS, CUMSUM_ROWS)),
        ],
        out_specs=pl.BlockSpec((None, tb, D_MODEL), lambda b, j: (b, j, 0)),
        out_shape=jax.ShapeDtypeStruct((batch, seq, D_MODEL), BF16),
        scratch_shapes=[
            pltpu.VMEM((W_RWKV // LANES, LANES, LANES), F32),
            pltpu.VMEM((tb, W_RWKV), F32),
            pltpu.VMEM((QK_GLA // LANES, DV_GLA, LANES), F32),
        ],
        compiler_params=pltpu.CompilerParams(
            dimension_semantics=("parallel", "arbitrary"), vmem_limit_bytes=VMEM_LIMIT),
        name="mixers",
    )(*prepared, wc.reshape(batch, seq // CHUNK, W_RWKV), lw, lb, ones_blocks,
      zg.reshape(batch, seq, ZG_COLS), gk_up, gk_bias, norm_g, ltri)


def _ffn_kernel(x_ref, mix_ref, wo_ref, g2_ref, wg_ref, wu_ref, wd_ref, gf_ref, o_ref):
    h = x_ref[...] + _dot(mix_ref[...], wo_ref[...])
    n = h * lax.rsqrt(jnp.mean(h * h, axis=-1, keepdims=True) + RMS_EPS) * g2_ref[...]
    n = n.astype(BF16)
    gate = _dot(n, wg_ref[...])
    up = _dot(n, wu_ref[...])
    act = (gate * _sigmoid(gate) * up).astype(BF16)
    h = h + _dot(act, wd_ref[...])
    o_ref[...] = h * lax.rsqrt(jnp.mean(h * h, axis=-1, keepdims=True) + RMS_EPS) * gf_ref[...]


def _ffn(x2d, mix, wo, g2, wg, wu, wd, gf):
    n_tok = x2d.shape[0]
    tm = TM_PROJ
    return pl.pallas_call(
        _ffn_kernel,
        grid=(n_tok // tm,),
        in_specs=[
            pl.BlockSpec((tm, D_MODEL), lambda i: (i, 0)),
            pl.BlockSpec((tm, D_MODEL), lambda i: (i, 0)),
            _const_spec((D_MODEL, D_MODEL)),
            _const_spec((1, D_MODEL)),
            _const_spec((D_MODEL, D_FF)),
            _const_spec((D_MODEL, D_FF)),
            _const_spec((D_FF, D_MODEL)),
            _const_spec((1, D_MODEL)),
        ],
        out_specs=pl.BlockSpec((tm, D_MODEL), lambda i: (i, 0)),
        out_shape=jax.ShapeDtypeStruct((n_tok, D_MODEL), F32),
        compiler_params=pltpu.CompilerParams(
            dimension_semantics=("parallel",), vmem_limit_bytes=VMEM_LIMIT),
        name="outproj_ffn",
    )(x2d, mix, wo, g2, wg, wu, wd, gf)


def _pad_cols(w, width):
    return jnp.pad(w, ((0, 0), (0, width - w.shape[1])))


def _block_tri(tb):
    i = jnp.arange(tb)
    same = (i[:, None] // CHUNK) == (i[None, :] // CHUNK)
    return (same & (i[:, None] >= i[None, :])).astype(BF16)


def kernel(x, rms1_g, w_in, mu_shift, decay_base, decay_up, iclr_base, iclr_up, gate_up, k_k, k_a, r_k,
           lnx_w, lnx_b, gk_up, gk_bias, gla_norm_g, w_out, rms2_g, ffn_gate, ffn_up, ffn_down, final_g):
    batch, seq, _ = x.shape
    depth = w_in.shape[0]
    d_rwkv_in = 3 * W_RWKV + LORA_DECAY + LORA_ICLR + LORA_GATE
    d_gla_in = 2 * QK_GLA + 2 * W_GLA + LORA_GK
    row = lambda t: t.reshape(1, -1).astype(F32)

    ltri = _block_tri(CUMSUM_ROWS)
    hi = jnp.arange(2 * LANES) // HD_RWKV
    ones_blocks = (hi[:, None] == hi[None, :]).astype(BF16)

    assert depth == 1, "single-layer trunk: the final norm is fused into the layer's last kernel"
    l = 0
    h = x.reshape(batch * seq, D_MODEL)

    w_r = _pad_cols(w_in[l][:, :d_rwkv_in], ZR_COLS).astype(BF16)
    w_g = _pad_cols(w_in[l][:, d_rwkv_in:d_rwkv_in + d_gla_in], ZG_COLS).astype(BF16)

    def lora_rows(w, start):
        return jnp.zeros((LORA_PAD, W_RWKV), F32).at[start:start + w.shape[0]].set(w).astype(BF16)

    du = lora_rows(decay_up[l], 0)
    iu = lora_rows(iclr_up[l], LORA_DECAY)
    gu = lora_rows(gate_up[l], LORA_DECAY + LORA_ICLR)
    mu = _pad_cols(row(mu_shift[l]), ZR_COLS)
    *prepared, wc, zg = _prep(h, row(rms1_g[l]), w_r, w_g, mu, row(decay_base[l]), du, row(iclr_base[l]), iu, gu,
                              row(k_k[l]), row(k_a[l]), row(r_k[l]), ones_blocks, ltri, seq)
    gku = jnp.zeros((GK_PAD, QK_GLA), F32).at[0:LORA_GK].set(gk_up[l]).astype(BF16)
    mix = _mixers(prepared, wc, zg, row(lnx_w[l]), row(lnx_b[l]), ones_blocks,
                  gku, row(gk_bias[l]), row(gla_norm_g[l]), ltri, batch, seq)

    out = _ffn(h, mix.reshape(batch * seq, D_MODEL), w_out[l].astype(BF16), row(rms2_g[l]),
               ffn_gate[l].astype(BF16), ffn_up[l].astype(BF16), ffn_down[l].astype(BF16), row(final_g))
    return out.reshape(batch, seq, D_MODEL)
```

```python
import functools
import math

import jax
import jax.numpy as jnp
from jax import lax
from jax.experimental import pallas as pl
from jax.experimental.pallas import tpu as pltpu

F32 = jnp.float32
BF16 = jnp.bfloat16

D_MODEL = 1024
W_RWKV = 512
HD_RWKV = 64
LORA_DECAY = 32
LORA_ICLR = 32
LORA_GATE = 96
GN_EPS = 64e-5
W_GLA = 512
DV_GLA = 128
H_GLA = 4
DK_GLA = 64
LORA_GK = 16
GK_NORMALIZER = 16.0
GLA_EPS = 1e-5
D_FF = 2816
RMS_EPS = 1e-6
CHUNK = 64

LANES = 128
PAIR_ROWS = 2 * CHUNK
LORA_PAD = 256
ZR_COLS = 3 * W_RWKV + LORA_PAD
GK_PAD = 128
ZG_COLS = 2 * H_GLA * DK_GLA + 2 * W_GLA + GK_PAD
QK_GLA = H_GLA * DK_GLA

TM_PROJ = 512
TB_MIX = 512
CHUNKS_PER_GROUP = 4
CUMSUM_ROWS = 256
VMEM_LIMIT = 56 * 1024 * 1024


def _dot(a, b):
    return jnp.dot(a, b, preferred_element_type=F32)


def _dot_nt(a, b):
    return lax.dot_general(a, b, (((1,), (1,)), ((), ())), preferred_element_type=F32)


def _dot_tn(a, b):
    return lax.dot_general(a, b, (((0,), (0,)), ((), ())), preferred_element_type=F32)


def _dot_01_rhs(m01, x):
    hi = x.astype(BF16)
    lo = (x - hi.astype(F32)).astype(BF16)
    return _dot(m01, hi) + _dot(m01, lo)


def _head_sums(x, ones_blocks):
    xb = x.astype(BF16)
    width = ones_blocks.shape[0]
    return jnp.concatenate([_dot(xb[:, c:c + width], ones_blocks) for c in range(0, x.shape[1], width)], axis=1)


def _sigmoid(x):
    return 1.0 / (1.0 + jnp.exp(-x))


def _softplus(x):
    return jnp.maximum(x, 0.0) + jnp.log(1.0 + jnp.exp(-jnp.abs(x)))


def _stack_heads(x, lane_is_h0):
    zero = jnp.zeros_like(x)
    return jnp.concatenate([jnp.where(lane_is_h0, x, zero), jnp.where(lane_is_h0, zero, x)], axis=0)


def _prep_kernel(x_ref, g1_ref, wr_ref, wg_ref, mu_ref, db_ref, du_ref, ib_ref, iu_ref, gu_ref, kk_ref, ka_ref,
                 rk_ref, ones_ref, ltri_ref,
                 at_ref, rt_ref, bc_ref, kc_ref, bh_ref, kh_ref, v_ref, gate_ref, bonus_ref, wc_ref, zg_ref,
                 zprev_ref, *, tiles_per_seq):
    tm = x_ref.shape[0]
    half = ltri_ref.shape[0]

    @pl.when(pl.program_id(0) % tiles_per_seq == 0)
    def _():
        zprev_ref[...] = jnp.zeros_like(zprev_ref)

    x = x_ref[...]
    n = x * lax.rsqrt(jnp.mean(x * x, axis=-1, keepdims=True) + RMS_EPS) * g1_ref[...]
    n = n.astype(BF16)
    ones_blocks = ones_ref[...]
    row = lax.broadcasted_iota(jnp.int32, (half, 1), 0)
    cols_of = {"r": (0, W_RWKV), "k": (W_RWKV, 2 * W_RWKV), "v": (2 * W_RWKV, 3 * W_RWKV),
               "lo": (3 * W_RWKV, ZR_COLS)}
    z = {}

    def project(name):
        c0, c1 = cols_of[name]
        z[name] = _dot(n, wr_ref[:, c0:c1])

    def gla_columns(c):
        w = min(2 * LANES, ZG_COLS - c)
        zg_ref[:, c:c + w] = _dot(n, wg_ref[:, c:c + w]).astype(zg_ref.dtype)

    def shifted(name, r0):
        c0, c1 = cols_of[name]
        cur = z[name][r0:r0 + half]
        carry = zprev_ref[:, c0:c1] if r0 == 0 else z[name][r0 - 1:r0]
        prev = jnp.where(row == 0, carry, pltpu.roll(cur, 1, 0))
        return cur + (prev - cur) * mu_ref[:, c0:c1]

    keep = [{} for _ in range(tm // half)]

    def decay_part(h):
        r0, rows, st = h * half, slice(h * half, (h + 1) * half), keep[h]
        lo = shifted("lo", r0)
        xw = db_ref[...] + _dot(jnp.tanh(lo).astype(BF16), du_ref[...])
        ld = _sigmoid(xw) * (-math.exp(-0.5))
        st["a"] = _sigmoid(ib_ref[...] + _dot(lo.astype(BF16), iu_ref[...]))
        gate_ref[rows, :] = _dot(_sigmoid(lo).astype(BF16), gu_ref[...]).astype(gate_ref.dtype)
        cw = _dot_01_rhs(ltri_ref[...], ld)
        st["e_cw"] = jnp.exp(cw)
        st["e_neg"] = jnp.exp(-cw)
        st["e_prev"] = jnp.exp(cw - ld)
        for c0 in range(0, half, CHUNK):
            wc_ref[(r0 + c0) // CHUNK:(r0 + c0) // CHUNK + 1, :] = st["e_cw"][c0 + CHUNK - 1:c0 + CHUNK, :]

    def key_part(h):
        r0, rows, st = h * half, slice(h * half, (h + 1) * half), keep[h]
        k = shifted("k", r0)
        kk = k * kk_ref[...]
        kk = kk * lax.rsqrt(jnp.maximum(_head_sums(kk * kk, ones_blocks), 1e-24))
        st["k2"] = k2 = k * (1.0 + (st["a"] - 1.0) * ka_ref[...])
        at_ref[rows, :] = (-kk * st["e_prev"]).astype(BF16)
        bnc = kk * st["a"] * st["e_neg"]
        knc = k2 * st["e_neg"]
        bc_ref[rows, :] = bnc.astype(BF16)
        kc_ref[rows, :] = knc.astype(BF16)
        for c0 in range(0, half, CHUNK):
            w_c = st["e_cw"][c0 + CHUNK - 1:c0 + CHUNK, :]
            crows = slice(r0 + c0, r0 + c0 + CHUNK)
            bh_ref[crows, :] = (bnc[c0:c0 + CHUNK] * w_c).astype(BF16)
            kh_ref[crows, :] = (knc[c0:c0 + CHUNK] * w_c).astype(BF16)

    def receptance_part(h):
        rows, st = slice(h * half, (h + 1) * half), keep[h]
        r = shifted("r", h * half)
        rt_ref[rows, :] = (r * st["e_cw"]).astype(BF16)
        st["rk"] = _head_sums(r * st["k2"] * rk_ref[...], ones_blocks)

    def value_part(h):
        rows, st = slice(h * half, (h + 1) * half), keep[h]
        v = shifted("v", h * half)
        v_ref[rows, :] = v.astype(BF16)
        bonus_ref[rows, :] = (st["rk"] * v).astype(bonus_ref.dtype)

    halves = range(tm // half)
    token_work = ([functools.partial(decay_part, h) for h in halves] + [functools.partial(key_part, h) for h in halves]
                  + [functools.partial(receptance_part, h) for h in halves]
                  + [functools.partial(value_part, h) for h in halves])
    projections = ([functools.partial(project, name) for name in ("lo", "k", "r", "v")]
                   + [functools.partial(gla_columns, c) for c in range(0, ZG_COLS, 2 * LANES)])
    projections.pop(0)()
    projections.pop(0)()
    while token_work or projections:
        if token_work:
            token_work.pop(0)()
        if projections:
            projections.pop(0)()
    for name, (c0, c1) in cols_of.items():
        zprev_ref[:, c0:c1] = z[name][tm - 1:tm, :]


def _const_spec(shape):
    return pl.BlockSpec(shape, lambda *_: (0,) * len(shape), pipeline_mode=pl.Buffered(1))


def _prep(x2d, g1, w_r, w_g, mu, db, du, ib, iu, gu, k_k, k_a, r_k, ones_blocks, ltri, seq):
    n_tok = x2d.shape[0]
    tm = TM_PROJ
    row512 = _const_spec((1, W_RWKV))
    tile = lambda width: pl.BlockSpec((tm, width), lambda i: (i, 0))
    bf_out = jax.ShapeDtypeStruct((n_tok, W_RWKV), BF16)
    return pl.pallas_call(
        functools.partial(_prep_kernel, tiles_per_seq=seq // tm),
        grid=(n_tok // tm,),
        in_specs=[
            tile(D_MODEL),
            _const_spec((1, D_MODEL)),
            _const_spec((D_MODEL, ZR_COLS)),
            _const_spec((D_MODEL, ZG_COLS)),
            _const_spec((1, ZR_COLS)),
            row512, _const_spec((LORA_PAD, W_RWKV)),
            row512, _const_spec((LORA_PAD, W_RWKV)),
            _const_spec((LORA_PAD, W_RWKV)),
            row512, row512, row512,
            _const_spec((2 * LANES, 2 * LANES)),
            _const_spec((CUMSUM_ROWS, CUMSUM_ROWS)),
        ],
        out_specs=[tile(W_RWKV)] * 9 + [
            pl.BlockSpec((tm // CHUNK, W_RWKV), lambda i: (i, 0)),
            tile(ZG_COLS),
        ],
        out_shape=[bf_out] * 9 + [
            jax.ShapeDtypeStruct((n_tok // CHUNK, W_RWKV), F32),
            jax.ShapeDtypeStruct((n_tok, ZG_COLS), BF16),
        ],
        scratch_shapes=[pltpu.VMEM((1, ZR_COLS), F32)],
        compiler_params=pltpu.CompilerParams(
            dimension_semantics=("arbitrary",), vmem_limit_bytes=VMEM_LIMIT),
        name="inproj_prep",
    )(x2d, g1, w_r, w_g, mu, db, du, ib, iu, gu, k_k, k_a, r_k, ones_blocks, ltri)


def _rwkv_steps(chunk_ids, state, at_ref, rt_ref, bc_ref, kc_ref, bh_ref, kh_ref, v_ref, wc_ref, ybuf_ref):
    n_chunks = at_ref.shape[0] // CHUNK
    n_pairs = W_RWKV // LANES
    j = pl.program_id(1)

    lane_is_h0 = lax.broadcasted_iota(jnp.int32, (CHUNK, LANES), 1) < HD_RWKV
    ti = lax.broadcasted_iota(jnp.int32, (CHUNK, LANES), 0)
    tj = lax.broadcasted_iota(jnp.int32, (CHUNK, LANES), 1) % CHUNK
    strict_lower = ti > tj
    incl_lower = ti >= tj
    eye = (ti == tj).astype(F32)
    ri = lax.broadcasted_iota(jnp.int32, (PAIR_ROWS, LANES), 0)
    ci = lax.broadcasted_iota(jnp.int32, (PAIR_ROWS, LANES), 1)
    same_head = (ri < HD_RWKV) == (ci < HD_RWKV)
    stack = lambda x: _stack_heads(x, lane_is_h0)
    bf = lambda x: x.astype(BF16)

    units = [(n, p) for n in chunk_ids for p in range(n_pairs)]
    rows_of = lambda n: pl.ds(n * CHUNK, CHUNK)
    lanes_of = lambda p: pl.ds(p * LANES, LANES)
    load = lambda ref: [ref[rows_of(n), lanes_of(p)] for n, p in units]
    w = {}

    def cross_products():
        for name, ref in (("at", at_ref), ("rt", rt_ref), ("bc", bc_ref), ("kc", kc_ref), ("v", v_ref),
                          ("bh", bh_ref), ("kh", kh_ref)):
            w[name] = load(ref)
        w["x4"] = [_dot_nt(jnp.concatenate([a_, r_], axis=0), jnp.concatenate([stack(b_), stack(k_)], axis=0))
                   for a_, r_, b_, k_ in zip(w["at"], w["rt"], w["bc"], w["kc"])]

    def masks():
        x4 = w.pop("x4")
        w["a_ab"] = [jnp.where(strict_lower, x[0:CHUNK, 0:LANES], 0.0) for x in x4]
        w["a_k"] = [bf(jnp.concatenate([jnp.where(strict_lower, x[0:CHUNK, LANES:], 0.0),
                                        jnp.where(incl_lower, x[CHUNK:, LANES:], 0.0)], axis=0)) for x in x4]
        w["a_rb"] = [bf(jnp.where(incl_lower, x[CHUNK:, 0:LANES], 0.0)) for x in x4]

    def values_products():
        w["av_both"] = [_dot(a_, stack(v_)) for a_, v_ in zip(w.pop("a_k"), w["v"])]

    def power_2():
        a_ab = w.pop("a_ab")
        w["t_inv"] = [eye + a_ for a_ in a_ab]
        w["a_pow"] = [bf(_dot(a_, stack(a_))) for a_ in map(bf, a_ab)]

    def next_power():
        x2 = [_dot(a_, jnp.concatenate([stack(a_), stack(bf(t_))], axis=1)) for a_, t_ in zip(w["a_pow"], w["t_inv"])]
        w["t_inv"] = [t_ + x[:, LANES:] for t_, x in zip(w["t_inv"], x2)]
        w["a_pow"] = [bf(x[:, 0:LANES]) for x in x2]

    def inverse_applied():
        t_inv = [t_ + _dot(a_, stack(bf(t_))) for t_, a_ in zip(w.pop("t_inv"), w.pop("a_pow"))]
        pq = [_dot(bf(t_), jnp.concatenate([stack(a_), stack(bf(avb[0:CHUNK]))], axis=1))
              for t_, a_, avb in zip(t_inv, w["at"], w["av_both"])]
        w["p_m"] = [bf(x[:, 0:LANES]) for x in pq]
        w["q_m"] = [x[:, LANES:] for x in pq]

    def transitions():
        w["m_p"] = [bf(jnp.where(same_head, _dot_tn(p_, b_), 0.0)) for p_, b_ in zip(w["p_m"], w["bh"])]
        w["n_p"] = [jnp.where(same_head, _dot_tn(jnp.concatenate([bf(q_), v_], axis=0),
                                                 jnp.concatenate([b_, k_], axis=0)), 0.0)
                    for q_, v_, b_, k_ in zip(w["q_m"], w["v"], w["bh"], w["kh"])]

    def chunk_step(n):
        ids = [chunk_ids.index(n) * n_pairs + p for p in range(n_pairs)]
        decay = wc_ref[pl.ds(j * n_chunks + n, 1), :]
        s_bf = [bf(s) for s in state]
        state[:] = [s * decay[:, p * LANES:(p + 1) * LANES] + _dot(sb, w["m_p"][i]) + w["n_p"][i]
                    for p, (i, s, sb) in enumerate(zip(ids, state, s_bf))]
        uy = [_dot_nt(jnp.concatenate([w["p_m"][i], w["rt"][i]], axis=0), sb) for i, sb in zip(ids, s_bf)]
        u = [x[0:CHUNK] + w["q_m"][i] for i, x in zip(ids, uy)]
        y_c = [x[CHUNK:] + _dot(w["a_rb"][i], stack(bf(u_))) + w["av_both"][i][CHUNK:]
               for i, x, u_ in zip(ids, uy, u)]
        for p in range(n_pairs):
            ybuf_ref[rows_of(n), lanes_of(p)] = y_c[p]

    first = [cross_products, masks, values_products, power_2] + [next_power] * 4 + [inverse_applied, transitions]
    return first, [functools.partial(chunk_step, n) for n in chunk_ids]


def _rwkv_finish(chunk_ids, ybuf_ref, gate_ref, bonus_ref, lw_ref, lb_ref, ones_ref, y_ref):
    rows = pl.ds(chunk_ids[0] * CHUNK, len(chunk_ids) * CHUNK)
    ones_blocks = ones_ref[...]
    y = ybuf_ref[rows, :]
    inv_hd = 1.0 / HD_RWKV
    mean = _head_sums(y, ones_blocks) * inv_hd
    d = y - mean
    var = _head_sums(d * d, ones_blocks) * inv_hd
    yn = d * lax.rsqrt(var + GN_EPS) * lw_ref[...] + lb_ref[...]
    y_ref[rows, 0:W_RWKV] = ((yn + bonus_ref[rows, :].astype(F32))
                             * gate_ref[rows, :].astype(F32)).astype(y_ref.dtype)


def _gla_steps(z_ref, gu_ref, gb_ref, ng_ref, ltri_ref, y_ref, s_ref):
    tb = z_ref.shape[0]
    n_chunks = tb // CHUNK
    n_pairs = QK_GLA // LANES
    units = [(n, p) for n in range(n_chunks) for p in range(n_pairs)]
    lane_is_h0 = lax.broadcasted_iota(jnp.int32, (CHUNK, LANES), 1) < DK_GLA
    ri = lax.broadcasted_iota(jnp.int32, (PAIR_ROWS, CHUNK), 0)
    ci = lax.broadcasted_iota(jnp.int32, (PAIR_ROWS, CHUNK), 1)
    causal = jnp.where(ri < CHUNK, ri, ri - CHUNK) >= ci
    stack = lambda x: _stack_heads(x, lane_is_h0)
    w = {}

    def gate_logits():
        lo = z_ref[:, 2 * QK_GLA + 2 * W_GLA:ZG_COLS]
        x = _dot(lo.astype(BF16), gu_ref[...]) + gb_ref[...]
        w["gk"] = -_softplus(-x) * (1.0 / GK_NORMALIZER)

    def cumulative_gates():
        gk, span = w.pop("gk"), ltri_ref.shape[0]
        b = jnp.concatenate([_dot_01_rhs(ltri_ref[...], gk[r0:r0 + span]) for r0 in range(0, tb, span)],
                            axis=0)
        w["e_b"] = jnp.exp(b)
        w["e_nb"] = jnp.exp(-b)

    def scaled_operands():
        e_b, e_nb = w.pop("e_b"), w.pop("e_nb")
        q_in = (z_ref[:, 0:QK_GLA].astype(F32) * (DK_GLA ** -0.5) * e_b).astype(BF16)
        k_in = z_ref[:, QK_GLA:2 * QK_GLA].astype(F32) * e_nb
        k_in_bf = k_in.astype(BF16)
        v = z_ref[:, 2 * QK_GLA:2 * QK_GLA + W_GLA].astype(BF16)
        chunk = lambda t, n, c0, width: t[n * CHUNK:(n + 1) * CHUNK, c0:c0 + width]
        w["decay"] = [e_b[(n + 1) * CHUNK - 1:(n + 1) * CHUNK, :] for n in range(n_chunks)]
        w["q"] = [stack(chunk(q_in, n, p * LANES, LANES)) for n, p in units]
        w["k"] = [chunk(k_in_bf, n, p * LANES, LANES) for n, p in units]
        w["k_end"] = [stack((chunk(k_in, n, p * LANES, LANES)
                             * w["decay"][n][:, p * LANES:(p + 1) * LANES]).astype(BF16)) for n, p in units]
        w["v0"] = [chunk(v, n, (2 * p) * DV_GLA, DV_GLA) for n, p in units]
        w["v1"] = [chunk(v, n, (2 * p + 1) * DV_GLA, DV_GLA) for n, p in units]

    def scores():
        w["att"] = [jnp.where(causal, _dot_nt(q_, k_), 0.0).astype(BF16) for q_, k_ in zip(w["q"], w["k"])]

    def state_updates():
        ds = [_dot_tn(jnp.concatenate([v0, v1], axis=0), ke) for v0, v1, ke in zip(w["v0"], w["v1"], w["k_end"])]
        states = [s_ref[p] for p in range(n_pairs)]
        w["s_in"] = []
        for n in range(n_chunks):
            w["s_in"] += states
            states = [s * w["decay"][n][:, p * LANES:(p + 1) * LANES] + ds[n * n_pairs + p]
                      for p, s in enumerate(states)]
        for p in range(n_pairs):
            s_ref[p] = states[p]

    def outputs():
        intra0 = [_dot(a_[0:CHUNK], v_) for a_, v_ in zip(w["att"], w["v0"])]
        intra1 = [_dot(a_[CHUNK:], v_) for a_, v_ in zip(w["att"], w["v1"])]
        inter = [_dot_nt(q_, s_.astype(BF16)) for q_, s_ in zip(w["q"], w["s_in"])]
        w["o"] = {}
        for i, (n, p) in enumerate(units):
            w["o"][(n, 2 * p)] = intra0[i] + inter[i][0:CHUNK]
            w["o"][(n, 2 * p + 1)] = intra1[i] + inter[i][CHUNK:]

    def finish():
        ng = ng_ref[...]
        for h in range(H_GLA):
            o = jnp.concatenate([w["o"][(n, h)] for n in range(n_chunks)], axis=0)
            og = z_ref[:, pl.ds(2 * QK_GLA + W_GLA + h * DV_GLA, DV_GLA)].astype(F32)
            o = o * lax.rsqrt(jnp.mean(o * o, axis=-1, keepdims=True) + GLA_EPS) * ng
            y_ref[:, pl.ds(W_RWKV + h * DV_GLA, DV_GLA)] = (o * (og * _sigmoid(og))).astype(y_ref.dtype)

    return [gate_logits, cumulative_gates, scaled_operands], [scores, state_updates, outputs, finish]


def _mixer_kernel(at_ref, rt_ref, bc_ref, kc_ref, bh_ref, kh_ref, v_ref, gate_ref, bonus_ref, wc_ref,
                  lw_ref, lb_ref, ones_ref, zg_ref, gu_ref, gb_ref, ng_ref, ltri_ref,
                  y_ref,
                  s_rwkv_ref, ybuf_ref, s_gla_ref):
    @pl.when(pl.program_id(1) == 0)
    def _():
        s_rwkv_ref[...] = jnp.zeros_like(s_rwkv_ref)
        s_gla_ref[...] = jnp.zeros_like(s_gla_ref)

    n_chunks = at_ref.shape[0] // CHUNK
    n_pairs = W_RWKV // LANES
    state = [s_rwkv_ref[p] for p in range(n_pairs)]
    group_chunks = [list(range(c0, c0 + CHUNKS_PER_GROUP)) for c0 in range(0, n_chunks, CHUNKS_PER_GROUP)]
    groups = [_rwkv_steps(ids, state, at_ref, rt_ref, bc_ref, kc_ref, bh_ref, kh_ref, v_ref, wc_ref, ybuf_ref)
              for ids in group_chunks]
    finishes = [functools.partial(_rwkv_finish, ids, ybuf_ref, gate_ref, bonus_ref, lw_ref, lb_ref, ones_ref, y_ref)
                for ids in group_chunks]
    gla_tokens, gla_chunks = _gla_steps(zg_ref, gu_ref, gb_ref, ng_ref, ltri_ref, y_ref, s_gla_ref)

    main_work = [first for first, _ in groups] + [groups[-1][1]]
    side_work = [gla_tokens] + [chain for _, chain in groups[:-1]] + [[]]
    side_work[1] = side_work[1] + gla_chunks[:-1]
    side_work[-1] = side_work[-1] + gla_chunks[-1:]
    for group_a, group_b in zip(main_work, side_work):
        group_a, group_b = list(group_a), list(group_b)
        while group_a or group_b:
            if group_a:
                group_a.pop(0)()
            if group_b:
                group_b.pop(0)()
    for p in range(n_pairs):
        s_rwkv_ref[p] = state[p]
    for finish in finishes:
        finish()


def _mixers(prepared, wc, zg, lw, lb, ones_blocks, gk_up, gk_bias, norm_g, ltri, batch, seq):
    tb = TB_MIX
    row512 = _const_spec((1, W_RWKV))
    slab = pl.BlockSpec((None, tb, W_RWKV), lambda b, j: (b, j, 0))
    prepared = [t.reshape(batch, seq, W_RWKV) for t in prepared]
    return pl.pallas_call(
        _mixer_kernel,
        grid=(batch, seq // tb),
        in_specs=[slab] * 9 + [
            pl.BlockSpec((None, seq // CHUNK, W_RWKV), lambda b, j: (b, 0, 0)),
            row512, row512,
            _const_spec((2 * LANES, 2 * LANES)),
            pl.BlockSpec((None, tb, ZG_COLS), lambda b, j: (b, j, 0)),
            _const_spec((GK_PAD, QK_GLA)),
            _const_spec((1, QK_GLA)),
            _const_spec((1, DV_GLA)),
            _const_spec((CUMSUM_ROWS, CUMSUM_ROWS)),
        ],
        out_specs=pl.BlockSpec((None, tb, D_MODEL), lambda b, j: (b, j, 0)),
        out_shape=jax.ShapeDtypeStruct((batch, seq, D_MODEL), BF16),
        scratch_shapes=[
            pltpu.VMEM((W_RWKV // LANES, LANES, LANES), F32),
            pltpu.VMEM((tb, W_RWKV), F32),
            pltpu.VMEM((QK_GLA // LANES, DV_GLA, LANES), F32),
        ],
        compiler_params=pltpu.CompilerParams(
            dimension_semantics=("parallel", "arbitrary"), vmem_limit_bytes=VMEM_LIMIT),
        name="mixers",
    )(*prepared, wc.reshape(batch, seq // CHUNK, W_RWKV), lw, lb, ones_blocks,
      zg.reshape(batch, seq, ZG_COLS), gk_up, gk_bias, norm_g, ltri)


def _ffn_kernel(x_ref, mix_ref, wo_ref, g2_ref, wg_ref, wu_ref, wd_ref, gf_ref, o_ref):
    h = x_ref[...] + _dot(mix_ref[...], wo_ref[...])
    n = h * lax.rsqrt(jnp.mean(h * h, axis=-1, keepdims=True) + RMS_EPS) * g2_ref[...]
    n = n.astype(BF16)
    gate = _dot(n, wg_ref[...])
    up = _dot(n, wu_ref[...])
    act = (gate * _sigmoid(gate) * up).astype(BF16)
    h = h + _dot(act, wd_ref[...])
    o_ref[...] = h * lax.rsqrt(jnp.mean(h * h, axis=-1, keepdims=True) + RMS_EPS) * gf_ref[...]


def _ffn(x2d, mix, wo, g2, wg, wu, wd, gf):
    n_tok = x2d.shape[0]
    tm = TM_PROJ
    return pl.pallas_call(
        _ffn_kernel,
        grid=(n_tok // tm,),
        in_specs=[
            pl.BlockSpec((tm, D_MODEL), lambda i: (i, 0)),
            pl.BlockSpec((tm, D_MODEL), lambda i: (i, 0)),
            _const_spec((D_MODEL, D_MODEL)),
            _const_spec((1, D_MODEL)),
            _const_spec((D_MODEL, D_FF)),
            _const_spec((D_MODEL, D_FF)),
            _const_spec((D_FF, D_MODEL)),
            _const_spec((1, D_MODEL)),
        ],
        out_specs=pl.BlockSpec((tm, D_MODEL), lambda i: (i, 0)),
        out_shape=jax.ShapeDtypeStruct((n_tok, D_MODEL), F32),
        compiler_params=pltpu.CompilerParams(
            dimension_semantics=("parallel",), vmem_limit_bytes=VMEM_LIMIT),
        name="outproj_ffn",
    )(x2d, mix, wo, g2, wg, wu, wd, gf)


def _pad_cols(w, width):
    return jnp.pad(w, ((0, 0), (0, width - w.shape[1])))


def _block_tri(tb):
    i = jnp.arange(tb)
    same = (i[:, None] // CHUNK) == (i[None, :] // CHUNK)
    return (same & (i[:, None] >= i[None, :])).astype(BF16)


def kernel(x, rms1_g, w_in, mu_shift, decay_base, decay_up, iclr_base, iclr_up, gate_up, k_k, k_a, r_k,
           lnx_w, lnx_b, gk_up, gk_bias, gla_norm_g, w_out, rms2_g, ffn_gate, ffn_up, ffn_down, final_g):
    batch, seq, _ = x.shape
    depth = w_in.shape[0]
    d_rwkv_in = 3 * W_RWKV + LORA_DECAY + LORA_ICLR + LORA_GATE
    d_gla_in = 2 * QK_GLA + 2 * W_GLA + LORA_GK
    row = lambda t: t.reshape(1, -1).astype(F32)

    ltri = _block_tri(CUMSUM_ROWS)
    hi = jnp.arange(2 * LANES) // HD_RWKV
    ones_blocks = (hi[:, None] == hi[None, :]).astype(BF16)

    assert depth == 1, "single-layer trunk: the final norm is fused into the layer's last kernel"
    l = 0
    h = x.reshape(batch * seq, D_MODEL)

    w_r = _pad_cols(w_in[l][:, :d_rwkv_in].astype(BF16), ZR_COLS)
    w_g = _pad_cols(w_in[l][:, d_rwkv_in:d_rwkv_in + d_gla_in].astype(BF16), ZG_COLS)

    def lora_rows(w, start):
        return jnp.zeros((LORA_PAD, W_RWKV), F32).at[start:start + w.shape[0]].set(w).astype(BF16)

    du = lora_rows(decay_up[l], 0)
    iu = lora_rows(iclr_up[l], LORA_DECAY)
    gu = lora_rows(gate_up[l], LORA_DECAY + LORA_ICLR)
    mu = _pad_cols(row(mu_shift[l]), ZR_COLS)
    *prepared, wc, zg = _prep(h, row(rms1_g[l]), w_r, w_g, mu, row(decay_base[l]), du, row(iclr_base[l]), iu, gu,
                              row(k_k[l]), row(k_a[l]), row(r_k[l]), ones_blocks, ltri, seq)
    gku = jnp.zeros((GK_PAD, QK_GLA), F32).at[0:LORA_GK].set(gk_up[l]).astype(BF16)
    mix = _mixers(prepared, wc, zg, row(lnx_w[l]), row(lnx_b[l]), ones_blocks,
                  gku, row(gk_bias[l]), row(gla_norm_g[l]), ltri, batch, seq)

    out = _ffn(h, mix.reshape(batch * seq, D_MODEL), w_out[l].astype(BF16), row(rms2_g[l]),
               ffn_gate[l].astype(BF16), ffn_up[l].astype(BF16), ffn_down[l].astype(BF16), row(final_g))
    return out.reshape(batch, seq, D_MODEL)
```
